```python
import jax, jax.numpy as jnp
from jax import lax
import numpy as np

D_MODEL = 1024
BATCH = 4
SEQ = 4096
DEPTH = 4

GRID_W = 64
CTX_LEN = 256
EPS = 1e-6

RNN_WIDTH = D_MODEL
RNN_BLOCKS = 16
RNN_BLOCK = RNN_WIDTH // RNN_BLOCKS
CONV_W = 4
CONV_PAD_L = 2
LRU_C = 8.0

SGU_WIDTH = D_MODEL
SGU_CHUNK = 128
SGU_GROUPS = 8
SGU_GROUP_CH = SGU_WIDTH // SGU_GROUPS

N_HEADS = 16
N_KV_HEADS = 4
Q_PER_KV = N_HEADS // N_KV_HEADS
HEAD_DIM = 64
D_ATTN = N_HEADS * HEAD_DIM
WINDOW = 128
ATT_BLOCK = 128
BAND = ATT_BLOCK + 2 * WINDOW
ATT_SCALE = HEAD_DIM ** -0.5
ROPE_BASE = 10000.0
ROPE_FREQS = HEAD_DIM // 4

N_BRANCH = 3
D_FF = 4 * D_MODEL

OFF_A = 0
OFF_B = OFF_A + RNN_WIDTH
OFF_Q = OFF_B + 2 * SGU_WIDTH
OFF_K = OFF_Q + D_ATTN
OFF_V = OFF_K + N_KV_HEADS * HEAD_DIM
OFF_G = OFF_V + N_KV_HEADS * HEAD_DIM
IN_WIDTH = OFF_G + N_BRANCH * D_MODEL

kernel_name = 'hybrid_prefix_diffusion_block'


def rms_norm(t, g):
    tf = t.astype(jnp.float32)
    y = tf * lax.rsqrt(jnp.mean(tf * tf, axis=-1, keepdims=True) + EPS)
    return (y * g.astype(jnp.float32)).astype(t.dtype)


def layer_norm(t, g, b):
    tf = t.astype(jnp.float32)
    mu = jnp.mean(tf, axis=-1, keepdims=True)
    var = jnp.mean(jnp.square(tf - mu), axis=-1, keepdims=True)
    y = (tf - mu) * lax.rsqrt(var + EPS) * g.astype(jnp.float32) + b.astype(jnp.float32)
    return y.astype(t.dtype)


def modulate(h, shift, scale):
    return h * (1 + scale) + shift


def short_conv(t, w, b):
    L = t.shape[1]
    tp = jnp.pad(t, ((0, 0), (CONV_PAD_L, CONV_W - 1 - CONV_PAD_L), (0, 0)))
    out = b
    for k in range(CONV_W):
        out = out + w[k] * tp[:, k:k + L]
    return out


def lru_coeffs(t, wa, ba, wx, bx, lam):
    nb, L, C = t.shape
    tb = t.reshape(nb, L, RNN_BLOCKS, RNN_BLOCK)
    r = jax.nn.sigmoid(jnp.einsum('blhi,hij->blhj', tb, wa).reshape(nb, L, C) + ba)
    i = jax.nn.sigmoid(jnp.einsum('blhi,hij->blhj', tb, wx).reshape(nb, L, C) + bx)
    log_a = -LRU_C * r * jax.nn.softplus(-lam)
    a = jnp.exp(log_a)
    u = t * i * jnp.sqrt(-jnp.expm1(2 * log_a))
    return a, u


def linear_scan(a, u, h0):
    u = u.at[:, 0].add(a[:, 0] * h0)

    def combine(left, right):
        al, ul = left
        ar, ur = right
        return al * ar, ar * ul + ur

    _, h = lax.associative_scan(combine, (a, u), axis=1)
    return h


def rglru_bidir(xa, h0_f, h0_b, conv_w, conv_b, wa, ba, wx, bx, lam):
    xc = short_conv(xa, conv_w, conv_b)
    a_f, u_f = lru_coeffs(xc, wa[0], ba[0], wx[0], bx[0], lam[0])
    a_b, u_b = lru_coeffs(xc, wa[1], ba[1], wx[1], bx[1], lam[1])
    h_f = linear_scan(a_f, u_f, h0_f)
    h_b = jnp.flip(linear_scan(jnp.flip(a_b, 1), jnp.flip(u_b, 1), h0_b), 1)
    return h_f + h_b, h_f[:, -1], h_b[:, 0]


def spatial_gating(z, ln_g, ln_b, w_s, b_s):
    nb, L, _ = z.shape
    u, v = jnp.split(z, 2, axis=-1)
    v = layer_norm(v, ln_g, ln_b)
    nc = L // SGU_CHUNK
    vb = v.reshape(nb, nc, SGU_CHUNK, SGU_GROUPS, SGU_GROUP_CH)
    mixed = jnp.einsum('gpq,bnqgc->bnpgc', w_s, vb) + b_s.T[None, None, :, :, None]
    return u * mixed.reshape(nb, L, SGU_WIDTH)


def axial_rope_tables(rows, dtype):
    row = jnp.repeat(jnp.arange(rows), GRID_W).astype(jnp.float32)
    col = jnp.tile(jnp.arange(GRID_W), rows).astype(jnp.float32)
    inv = jnp.power(ROPE_BASE, -jnp.arange(ROPE_FREQS, dtype=jnp.float32) / ROPE_FREQS)
    ang = jnp.concatenate([row[:, None] * inv, col[:, None] * inv], axis=-1)
    return jnp.cos(ang).astype(dtype), jnp.sin(ang).astype(dtype)


def apply_axial_rope(t, cos, sin):
    shp = t.shape
    tr = t.reshape(*shp[:-1], 2, 2, ROPE_FREQS)
    t1, t2 = tr[..., 0, :], tr[..., 1, :]
    cs = cos.reshape(shp[1], 2, ROPE_FREQS)[None, :, None]
    sn = sin.reshape(shp[1], 2, ROPE_FREQS)[None, :, None]
    return jnp.stack([t1 * cs - t2 * sn, t2 * cs + t1 * sn], axis=-2).reshape(shp)


def sink_attend(q, ks, vs, sink, mask=None):
    logits = [jnp.einsum('bqkgd,bjkd->bkgqj', q, k).astype(jnp.float32) for k in ks]
    if mask is not None:
        logits[0] = jnp.where(mask, logits[0], -jnp.inf)
    sink_col = jnp.broadcast_to(sink.reshape(N_KV_HEADS, Q_PER_KV, 1, 1).astype(jnp.float32),
                                logits[0].shape[:-1] + (1,))
    p = jax.nn.softmax(jnp.concatenate(logits + [sink_col], axis=-1), axis=-1)
    out = None
    start = 0
    for k, v in zip(ks, vs):
        n_k = k.shape[1]
        part = jnp.einsum('bkgqj,bjkd->bqkgd', p[..., start:start + n_k].astype(v.dtype), v)
        out = part if out is None else out + part
        start += n_k
    return out


def windowed_attention(q, k, v, k_ctx, v_ctx, sink):
    nb, S, _, _ = q.shape
    n_blk = S // ATT_BLOCK
    qb = jnp.moveaxis(q.reshape(nb, n_blk, ATT_BLOCK, N_KV_HEADS, Q_PER_KV, HEAD_DIM), 1, 0)
    pad = ((0, 0), (WINDOW, WINDOW), (0, 0), (0, 0))
    kp = jnp.pad(k, pad)
    vp = jnp.pad(v, pad)
    i = jnp.arange(ATT_BLOCK)[:, None]
    j = jnp.arange(BAND)[None, :]
    in_window = jnp.abs(j - WINDOW - i) <= WINDOW

    def block(args):
        n, q_n = args
        k_n = lax.dynamic_slice_in_dim(kp, n * ATT_BLOCK, BAND, axis=1)
        v_n = lax.dynamic_slice_in_dim(vp, n * ATT_BLOCK, BAND, axis=1)
        tok = n * ATT_BLOCK - WINDOW + j
        mask = in_window & (tok >= 0) & (tok < S)
        return sink_attend(q_n, [k_n, k_ctx], [v_n, v_ctx], sink, mask)

    out = lax.map(block, (jnp.arange(n_blk), qb))
    return jnp.moveaxis(out, 0, 1).reshape(nb, S, D_ATTN)


def merge_branches(gate_logits, ya, yb, yc, w_branch, w_out):
    ga, gb, gc = jnp.split(jax.nn.sigmoid(gate_logits), N_BRANCH, axis=-1)
    m = ga * (ya @ w_branch[0]) + gb * (yb @ w_branch[1]) + gc * (yc @ w_branch[2])
    return m @ w_out


def sq_relu_mlp(h, w1, w2):
    return jnp.square(jax.nn.relu(h @ w1)) @ w2


def setup_inputs(seed: int = 0) -> dict:
    key = jax.random.key(seed)
    ks = jax.random.split(key, 27)

    def nrm(k, shape, scale):
        return jax.random.normal(k, shape, jnp.float32) * scale

    u = jax.random.uniform(ks[14], (DEPTH, 2, RNN_WIDTH), jnp.float32, 0.9, 0.999)
    s = u ** (1.0 / LRU_C)
    lru_lambda = jnp.log(s) - jnp.log1p(-s)
    return {
        'x': nrm(ks[0], (BATCH, SEQ, D_MODEL), 1.0),
        'c': nrm(ks[1], (BATCH, D_MODEL), 1.0),
        'ctx': nrm(ks[2], (BATCH, CTX_LEN, D_MODEL), 1.0),
        'c_ctx': nrm(ks[3], (D_MODEL,), 1.0),
        'w_mod': nrm(ks[4], (DEPTH, D_MODEL, 6 * D_MODEL), 0.5 * D_MODEL ** -0.5),
        'b_mod': nrm(ks[5], (DEPTH, 6 * D_MODEL), 0.01),
        'g_norm1': 1.0 + nrm(ks[6], (DEPTH, D_MODEL), 0.01),
        'w_in': nrm(ks[7], (DEPTH, D_MODEL, IN_WIDTH), D_MODEL ** -0.5),
        'conv_w': nrm(ks[8], (DEPTH, CONV_W, RNN_WIDTH), CONV_W ** -0.5),
        'conv_b': nrm(ks[9], (DEPTH, RNN_WIDTH), 0.01),
        'lru_wa': nrm(ks[10], (DEPTH, 2, RNN_BLOCKS, RNN_BLOCK, RNN_BLOCK), RNN_BLOCK ** -0.5),
        'lru_ba': nrm(ks[11], (DEPTH, 2, RNN_WIDTH), 0.01),
        'lru_wx': nrm(ks[12], (DEPTH, 2, RNN_BLOCKS, RNN_BLOCK, RNN_BLOCK), RNN_BLOCK ** -0.5),
        'lru_bx': nrm(ks[13], (DEPTH, 2, RNN_WIDTH), 0.01),
        'lru_lambda': lru_lambda,
        'sgu_ln_g': 1.0 + nrm(ks[15], (DEPTH, SGU_WIDTH), 0.01),
        'sgu_ln_b': nrm(ks[16], (DEPTH, SGU_WIDTH), 0.01),
        'sgu_w': nrm(ks[17], (DEPTH, SGU_GROUPS, SGU_CHUNK, SGU_CHUNK), SGU_CHUNK ** -0.5),
        'sgu_b': 1.0 + nrm(ks[18], (DEPTH, SGU_GROUPS, SGU_CHUNK), 0.01),
        'q_norm_g': 1.0 + nrm(ks[19], (DEPTH, HEAD_DIM), 0.01),
        'k_norm_g': 1.0 + nrm(ks[20], (DEPTH, HEAD_DIM), 0.01),
        'sink': nrm(ks[21], (DEPTH, N_HEADS), 1.0),
        'w_branch': nrm(ks[22], (DEPTH, N_BRANCH, D_MODEL, D_MODEL), D_MODEL ** -0.5),
        'w_out': nrm(ks[23], (DEPTH, D_MODEL, D_MODEL), D_MODEL ** -0.5),
        'g_norm2': 1.0 + nrm(ks[24], (DEPTH, D_MODEL), 0.01),
        'w_ff1': nrm(ks[25], (DEPTH, D_MODEL, D_FF), D_MODEL ** -0.5),
        'w_ff2': nrm(ks[26], (DEPTH, D_FF, D_MODEL), D_FF ** -0.5),
    }


def reference(x, c, ctx, c_ctx, w_mod, b_mod, g_norm1, w_in, conv_w, conv_b, lru_wa, lru_ba,
              lru_wx, lru_bx, lru_lambda, sgu_ln_g, sgu_ln_b, sgu_w, sgu_b, q_norm_g, k_norm_g,
              sink, w_branch, w_out, g_norm2, w_ff1, w_ff2):
    n_batch, n_tok, _ = x.shape
    n_ctx = ctx.shape[1]
    ROWS = n_tok // GRID_W
    cos, sin = axial_rope_tables(ROWS, x.dtype)
    cond_x = jax.nn.silu(c)
    cond_c = jax.nn.silu(c_ctx)
    cx = ctx
    for l in range(DEPTH):
        last = l == DEPTH - 1
        mod_x = (cond_x @ w_mod[l] + b_mod[l])[:, None, :]
        mod_c = cond_c @ w_mod[l] + b_mod[l]
        sh1x, sc1x, ga1x, sh2x, sc2x, ga2x = jnp.split(mod_x, 6, axis=-1)
        sh1c, sc1c, ga1c, sh2c, sc2c, ga2c = jnp.split(mod_c, 6, axis=-1)
        lru_p = (conv_w[l], conv_b[l], lru_wa[l], lru_ba[l], lru_wx[l], lru_bx[l], lru_lambda[l])
        sgu_p = (sgu_ln_g[l], sgu_ln_b[l], sgu_w[l], sgu_b[l])

        zc = modulate(rms_norm(cx, g_norm1[l]), sh1c, sc1c) @ w_in[l]
        kc = rms_norm(zc[..., OFF_K:OFF_V].reshape(n_batch, n_ctx, N_KV_HEADS, HEAD_DIM), k_norm_g[l])
        vc = zc[..., OFF_V:OFF_G].reshape(n_batch, n_ctx, N_KV_HEADS, HEAD_DIM)
        h0 = jnp.zeros((n_batch, RNN_WIDTH), zc.dtype)
        ya_c, hf_c, hb_c = rglru_bidir(zc[..., OFF_A:OFF_B], h0, h0, *lru_p)

        zx = modulate(rms_norm(x, g_norm1[l]), sh1x, sc1x) @ w_in[l]
        ya_x, _, _ = rglru_bidir(zx[..., OFF_A:OFF_B], hf_c, hb_c, *lru_p)
        yb_x = spatial_gating(jax.nn.gelu(zx[..., OFF_B:OFF_Q]), *sgu_p)
        q = rms_norm(zx[..., OFF_Q:OFF_K].reshape(n_batch, n_tok, N_HEADS, HEAD_DIM), q_norm_g[l])
        k = rms_norm(zx[..., OFF_K:OFF_V].reshape(n_batch, n_tok, N_KV_HEADS, HEAD_DIM), k_norm_g[l])
        v = zx[..., OFF_V:OFF_G].reshape(n_batch, n_tok, N_KV_HEADS, HEAD_DIM)
        q = apply_axial_rope(q, cos, sin) * ATT_SCALE
        k = apply_axial_rope(k, cos, sin)
        yc_x = windowed_attention(q, k, v, kc, vc, sink[l])
        x = x + ga1x * merge_branches(zx[..., OFF_G:], ya_x, yb_x, yc_x, w_branch[l], w_out[l])
        x = x + ga2x * sq_relu_mlp(modulate(rms_norm(x, g_norm2[l]), sh2x, sc2x), w_ff1[l], w_ff2[l])

        if not last:
            yb_c = spatial_gating(jax.nn.gelu(zc[..., OFF_B:OFF_Q]), *sgu_p)
            qc = rms_norm(zc[..., OFF_Q:OFF_K].reshape(n_batch, n_ctx, N_KV_HEADS, Q_PER_KV, HEAD_DIM),
                          q_norm_g[l]) * ATT_SCALE
            yc_c = sink_attend(qc, [kc], [vc], sink[l]).reshape(n_batch, n_ctx, D_ATTN)
            cx = cx + ga1c * merge_branches(zc[..., OFF_G:], ya_c, yb_c, yc_c, w_branch[l], w_out[l])
            cx = cx + ga2c * sq_relu_mlp(modulate(rms_norm(cx, g_norm2[l]), sh2c, sc2c), w_ff1[l], w_ff2[l])
    return x
```

```python
import functools

import numpy as np
import jax
import jax.numpy as jnp
from jax import lax
from jax.experimental import pallas as pl
from jax.experimental.pallas import tpu as pltpu

F32 = jnp.float32
BF16 = jnp.bfloat16

D_MODEL = 1024
GRID_W = 64
EPS = 1e-6
RNN_BLOCKS = 16
RNN_BLOCK = D_MODEL // RNN_BLOCKS
CONV_W = 4
CONV_PAD_L = 2
LRU_C = 8.0
SGU_CHUNK = 128
SGU_GROUPS = 8
N_HEADS = 16
N_KV_HEADS = 4
HEAD_DIM = 64
WINDOW = 128
ATT_BLOCK = 128
ATT_SCALE = HEAD_DIM ** -0.5
ROPE_BASE = 10000.0
ROPE_FREQS = HEAD_DIM // 4
D_FF = 4 * D_MODEL
OFF_B = D_MODEL
OFF_Q = OFF_B + 2 * D_MODEL
OFF_K = OFF_Q + N_HEADS * HEAD_DIM
OFF_V = OFF_K + N_KV_HEADS * HEAD_DIM
OFF_G = OFF_V + N_KV_HEADS * HEAD_DIM

LANES = 128
SUBLANES = 8
MXU_DIM = 256
VMEM_BYTES = 64 * 1024 * 1024

N_SLABS = D_MODEL // LANES
MOD_ROWS = SUBLANES
NEG_BIG = -1e30


def _vmem_limit(nbytes):
    return int(min(nbytes, VMEM_BYTES - 6 * 1024 * 1024))


def _rms(x, g):
    ms = jnp.mean(x * x, axis=-1, keepdims=True)
    return x * lax.rsqrt(ms + EPS) * g


def _gelu_tanh(x):
    c = np.sqrt(2.0 / np.pi).astype(np.float32)
    return x * (0.5 * (1.0 + jnp.tanh(c * (x + 0.044715 * (x * x * x)))))


def _dot(a, b):
    return jnp.dot(a, b, preferred_element_type=F32)


def _mod_kernel(c_ref, w_ref, b_ref, o_ref):
    c = c_ref[...]
    s = c * jax.nn.sigmoid(c)
    o_ref[0] = _dot(s.astype(BF16), w_ref[0].astype(BF16)) + b_ref[0]


def _modulation(cond, w_mod, b_mod):
    depth, _, width = w_mod.shape
    tn = 1536
    return pl.pallas_call(
        _mod_kernel,
        grid=(depth, width // tn),
        in_specs=[
            pl.BlockSpec((MOD_ROWS, D_MODEL), lambda l, j: (0, 0)),
            pl.BlockSpec((1, D_MODEL, tn), lambda l, j: (l, 0, j)),
            pl.BlockSpec((1, 1, tn), lambda l, j: (l, 0, j)),
        ],
        out_specs=pl.BlockSpec((1, MOD_ROWS, tn), lambda l, j: (l, 0, j)),
        out_shape=jax.ShapeDtypeStruct((depth, MOD_ROWS, width), F32),
        compiler_params=pltpu.CompilerParams(vmem_limit_bytes=_vmem_limit(40 << 20)),
        name="modulation",
    )(cond, w_mod, b_mod.reshape(depth, 1, width))


def _inproj_kernel(x_ref, mod_ref, g1_ref, w_ref, lng_ref, lnb_ref, ws_ref, bs_ref, qg_ref, kg_ref,
                   ones_ref, cq_ref, sq_ref, ck_ref, sk_ref,
                   xa_ref, yb_ref, q_ref, kd_ref, vd_ref, *, tm, mod_base, tiles_per_row):
    i = pl.program_id(0)
    row = mod_base + i // tiles_per_row
    shift = mod_ref[pl.ds(row, 1), 0:D_MODEL]
    scale = mod_ref[pl.ds(row, 1), D_MODEL:2 * D_MODEL]
    h = (_rms(x_ref[...], g1_ref[...]) * (1.0 + scale) + shift).astype(BF16)

    xa_ref[...] = _dot(h, w_ref[:, 0:OFF_B])

    u = _gelu_tanh(_dot(h, w_ref[:, OFF_B:OFF_B + D_MODEL]))
    v = _gelu_tanh(_dot(h, w_ref[:, OFF_B + D_MODEL:OFF_Q]))
    mu = jnp.mean(v, axis=-1, keepdims=True)
    vc = v - mu
    var = jnp.mean(vc * vc, axis=-1, keepdims=True)
    vn = (vc * lax.rsqrt(var + EPS) * lng_ref[...] + lnb_ref[...]).astype(BF16)
    for c in range(tm // SGU_CHUNK):
        rows = slice(c * SGU_CHUNK, (c + 1) * SGU_CHUNK)
        for g in range(SGU_GROUPS):
            cols = slice(g * LANES, (g + 1) * LANES)
            mixed = _dot(ws_ref[g], vn[rows, cols]) + bs_ref[g]
            yb_ref[rows, cols] = (u[rows, cols] * mixed).astype(BF16)

    lane = lax.broadcasted_iota(jnp.int32, (tm, LANES), 1)
    first_half = (lane % (2 * ROPE_FREQS)) < ROPE_FREQS
    low_head = lane < HEAD_DIM

    def head_norm_rope(z, g_ref, c_ref, s_ref, out_ref):
        zz =(z * z).astype(BF16)
        for blk in range(z.shape[1] // MXU_DIM):
            cols = slice(blk * MXU_DIM, (blk + 1) * MXU_DIM)
            ms = _dot(zz[:, cols], ones_ref[...]) * (1.0 / HEAD_DIM)
            zn = z[:, cols] * lax.rsqrt(ms + EPS)
            for s in range(MXU_DIM // LANES):
                t = zn[:, s * LANES:(s + 1) * LANES] * g_ref[...]
                sw = jnp.where(first_half, pltpu.roll(t, LANES - ROPE_FREQS, 1),
                               pltpu.roll(t, ROPE_FREQS, 1))
                slab = blk * (MXU_DIM // LANES) + s
                out_ref(slab, t * c_ref[...] + sw * s_ref[...])

    def store_q(slab, val):
        q_ref[:, slab * LANES:(slab + 1) * LANES] = val.astype(BF16)

    def dup_heads(slab_val):
        r = pltpu.roll(slab_val, HEAD_DIM, 1)
        return jnp.where(low_head, slab_val, r), jnp.where(low_head, r, slab_val)

    def store_kd(slab, val):
        a, b = dup_heads(val)
        kd_ref[:, (2 * slab) * LANES:(2 * slab + 1) * LANES] = a.astype(BF16)
        kd_ref[:, (2 * slab + 1) * LANES:(2 * slab + 2) * LANES] = b.astype(BF16)

    head_norm_rope(_dot(h, w_ref[:, OFF_Q:OFF_K]), qg_ref, cq_ref, sq_ref, store_q)
    head_norm_rope(_dot(h, w_ref[:, OFF_K:OFF_V]), kg_ref, ck_ref, sk_ref, store_kd)

    vv = _dot(h, w_ref[:, OFF_V:OFF_G])
    for s in range((OFF_G - OFF_V) // LANES):
        a, b = dup_heads(vv[:, s * LANES:(s + 1) * LANES])
        vd_ref[:, (2 * s) * LANES:(2 * s + 1) * LANES] = a.astype(BF16)
        vd_ref[:, (2 * s + 1) * LANES:(2 * s + 2) * LANES] = b.astype(BF16)


def _inproj(x2d, mod_l, g1, w_a, lng, lnb, ws, bs, qg, kg, ones_bd, tabs, *, seq, mod_base, per_batch):
    n_tok = x2d.shape[0]
    tm = 256
    tiles_per_seq = seq // tm
    tiles_per_row = tiles_per_seq if per_batch else n_tok // tm
    cq, sq, ck, sk = tabs
    const = lambda i: (0, 0)
    tab_map = lambda i: (i % tiles_per_seq, 0)
    kvw = 2 * N_KV_HEADS * HEAD_DIM
    return pl.pallas_call(
        functools.partial(_inproj_kernel, tm=tm, mod_base=mod_base, tiles_per_row=tiles_per_row),
        grid=(n_tok // tm,),
        in_specs=[
            pl.BlockSpec((tm, D_MODEL), lambda i: (i, 0)),
            pl.BlockSpec((MOD_ROWS, 6 * D_MODEL), const),
            pl.BlockSpec((1, D_MODEL), const),
            pl.BlockSpec((D_MODEL, OFF_G), const),
            pl.BlockSpec((1, D_MODEL), const),
            pl.BlockSpec((1, D_MODEL), const),
            pl.BlockSpec((SGU_GROUPS, SGU_CHUNK, SGU_CHUNK), lambda i: (0, 0, 0)),
            pl.BlockSpec((SGU_GROUPS, SGU_CHUNK, LANES), lambda i: (0, 0, 0)),
            pl.BlockSpec((1, LANES), const),
            pl.BlockSpec((1, LANES), const),
            pl.BlockSpec((MXU_DIM, MXU_DIM), const),
            pl.BlockSpec((tm, LANES), tab_map),
            pl.BlockSpec((tm, LANES), tab_map),
            pl.BlockSpec((tm, LANES), tab_map),
            pl.BlockSpec((tm, LANES), tab_map),
        ],
        out_specs=[
            pl.BlockSpec((tm, D_MODEL), lambda i: (i, 0)),
            pl.BlockSpec((tm, D_MODEL), lambda i: (i, 0)),
            pl.BlockSpec((tm, D_MODEL), lambda i: (i, 0)),
            pl.BlockSpec((tm, kvw), lambda i: (i, 0)),
            pl.BlockSpec((tm, kvw), lambda i: (i, 0)),
        ],
        out_shape=[
            jax.ShapeDtypeStruct((n_tok, D_MODEL), F32),
            jax.ShapeDtypeStruct((n_tok, D_MODEL), BF16),
            jax.ShapeDtypeStruct((n_tok, D_MODEL), BF16),
            jax.ShapeDtypeStruct((n_tok, kvw), BF16),
            jax.ShapeDtypeStruct((n_tok, kvw), BF16),
        ],
        compiler_params=pltpu.CompilerParams(vmem_limit_bytes=_vmem_limit(52 << 20)),
        name="inproj",
    )(x2d, mod_l, g1, w_a, lng, lnb, ws, bs, qg, kg, ones_bd, cq, sq, ck, sk)


def _scan_kernel(xa_ref, xp_ref, xn_ref, cw_ref, cb_ref, wa_ref, wx_ref, ba_ref, bx_ref, lam_ref, h0_ref,
                 out_ref, hfin_ref, xe_ref, a_ref, u_ref, hc_ref, *, tc, nchunk, nb):
    d = pl.program_id(0)
    i = pl.program_id(1)
    c = jnp.where(d == 0, i, nchunk - 1 - i)

    @pl.when(i == 0)
    def _():
        hc_ref[...] = h0_ref[0]

    z = -lam_ref[0]
    softplus = jnp.maximum(z, 0.0) + jnp.log1p(jnp.exp(-jnp.abs(z)))
    has_prev = (c > 0).astype(F32)
    has_next = (c < nchunk - 1).astype(F32)

    for b in range(nb):
        xe_ref[0:SUBLANES, :] = xp_ref[b] * has_prev
        xe_ref[SUBLANES:SUBLANES + tc, :] = xa_ref[b]
        xe_ref[SUBLANES + tc:2 * SUBLANES + tc, :] = xn_ref[b] * has_next
        xc = cb_ref[...]
        for k in range(CONV_W):
            xc = xc + cw_ref[k:k + 1, :] * xe_ref[pl.ds(SUBLANES + k - CONV_PAD_L, tc), :]
        for blk in range(D_MODEL // MXU_DIM):
            cols = slice(blk * MXU_DIM, (blk + 1) * MXU_DIM)
            xcb = xc[:, cols]
            xcb16 = xcb.astype(BF16)
            r = jax.nn.sigmoid(_dot(xcb16, wa_ref[0, blk]) + ba_ref[0, :, cols])
            ig = jax.nn.sigmoid(_dot(xcb16, wx_ref[0, blk]) + bx_ref[0, :, cols])
            log_a = (-LRU_C) * r * softplus[:, cols]
            a = jnp.exp(log_a)
            u = xcb * ig * jnp.sqrt(-jnp.tanh(log_a) * (a * a + 1.0))
            for s in range(MXU_DIM // LANES):
                slab = blk * (MXU_DIM // LANES) + s
                a_ref[b, pl.ds(slab, tc, stride=N_SLABS), :] = a[:, s * LANES:(s + 1) * LANES]
                u_ref[b, pl.ds(slab, tc, stride=N_SLABS), :] = u[:, s * LANES:(s + 1) * LANES]

    def step(t, hs):
        tt = jnp.where(d == 0, t, tc - 1 - t)
        off = pl.multiple_of(tt * N_SLABS, N_SLABS)
        new = []
        for b in range(nb):
            hb = a_ref[b, pl.ds(off, N_SLABS), :] * hs[b] + u_ref[b, pl.ds(off, N_SLABS), :]
            u_ref[b, pl.ds(off, N_SLABS), :] = hb
            new.append(hb)
        return tuple(new)

    hs = lax.fori_loop(0, tc, step, tuple(hc_ref[b] for b in range(nb)), unroll=8)
    for b in range(nb):
        hc_ref[b] = hs[b]
        hfin_ref[0, b] = hs[b]
        for slab in range(N_SLABS):
            out_ref[0, b, :, slab * LANES:(slab + 1) * LANES] = (
                u_ref[b, pl.ds(slab, tc, stride=N_SLABS), :].astype(BF16))


def _bidir_scan(xa, cw, cb, wa_bd, wx_bd, ba, bx, lam, h0):
    nb, seq, _ = xa.shape
    tc = 256
    nchunk = seq // tc
    halo_blocks = seq // SUBLANES
    chunk = lambda d, i: i + d * (nchunk - 1 - 2 * i)
    nblk = D_MODEL // MXU_DIM
    return pl.pallas_call(
        functools.partial(_scan_kernel, tc=tc, nchunk=nchunk, nb=nb),
        grid=(2, nchunk),
        in_specs=[
            pl.BlockSpec((nb, tc, D_MODEL), lambda d, i: (0, chunk(d, i), 0)),
            pl.BlockSpec((nb, SUBLANES, D_MODEL),
                         lambda d, i: (0, jnp.maximum(chunk(d, i) * (tc // SUBLANES) - 1, 0), 0)),
            pl.BlockSpec((nb, SUBLANES, D_MODEL),
                         lambda d, i: (0, jnp.minimum((chunk(d, i) + 1) * (tc // SUBLANES), halo_blocks - 1), 0)),
            pl.BlockSpec((CONV_W, D_MODEL), lambda d, i: (0, 0)),
            pl.BlockSpec((1, D_MODEL), lambda d, i: (0, 0)),
            pl.BlockSpec((1, nblk, MXU_DIM, MXU_DIM), lambda d, i: (d, 0, 0, 0)),
            pl.BlockSpec((1, nblk, MXU_DIM, MXU_DIM), lambda d, i: (d, 0, 0, 0)),
            pl.BlockSpec((1, 1, D_MODEL), lambda d, i: (d, 0, 0)),
            pl.BlockSpec((1, 1, D_MODEL), lambda d, i: (d, 0, 0)),
            pl.BlockSpec((1, 1, D_MODEL), lambda d, i: (d, 0, 0)),
            pl.BlockSpec((1, nb, N_SLABS, LANES), lambda d, i: (d, 0, 0, 0)),
        ],
        out_specs=[
            pl.BlockSpec((1, nb, tc, D_MODEL), lambda d, i: (d, 0, chunk(d, i), 0)),
            pl.BlockSpec((1, nb, N_SLABS, LANES), lambda d, i: (d, 0, 0, 0)),
        ],
        out_shape=[
            jax.ShapeDtypeStruct((2, nb, seq, D_MODEL), BF16),
            jax.ShapeDtypeStruct((2, nb, N_SLABS, LANES), F32),
        ],
        scratch_shapes=[
            pltpu.VMEM((tc + 2 * SUBLANES, D_MODEL), F32),
            pltpu.VMEM((nb, tc * N_SLABS, LANES), F32),
            pltpu.VMEM((nb, tc * N_SLABS, LANES), F32),
            pltpu.VMEM((nb, N_SLABS, LANES), F32),
        ],
        compiler_params=pltpu.CompilerParams(vmem_limit_bytes=_vmem_limit(52 << 20)),
        name="bidir_scan",
    )(xa, xa, xa, cw, cb, wa_bd, wx_bd, ba, bx, lam, h0)


def _attn_kernel(sink_ref, q_ref, *refs, nblk, band):
    if band:
        kp_ref, kc_ref_, kn_ref, vp_ref, vcur_ref, vn_ref, kx_ref, vx_ref, o_ref = refs
    else:
        kx_ref, vx_ref, o_ref = refs
    n = pl.program_id(1)
    rows = 2 * ATT_BLOCK
    lane = lax.broadcasted_iota(jnp.int32, (1, LANES), 1)
    low = lane < HEAD_DIM
    low_out = lax.broadcasted_iota(jnp.int32, (ATT_BLOCK, LANES), 1) < HEAD_DIM

    if band:
        qi = lax.broadcasted_iota(jnp.int32, (ATT_BLOCK, 3 * ATT_BLOCK), 0)
        kj = lax.broadcasted_iota(jnp.int32, (ATT_BLOCK, 3 * ATT_BLOCK), 1)
        lower = jnp.maximum(qi, jnp.where(n > 0, 0, ATT_BLOCK))
        upper = jnp.minimum(qi + (ATT_BLOCK + WINDOW), jnp.where(n < nblk - 1, 3 * ATT_BLOCK - 1, 2 * ATT_BLOCK - 1))
        ok = (kj >= lower) & (kj <= upper)
        bias1 = jnp.where(ok, 0.0, NEG_BIG).astype(F32)
        bias = jnp.concatenate([bias1, bias1], axis=0)

    for kh in range(N_KV_HEADS):
        slab = slice(kh * LANES, (kh + 1) * LANES)
        q2 = jnp.concatenate([q_ref[:, (2 * kh) * LANES:(2 * kh + 1) * LANES],
                              q_ref[:, (2 * kh + 1) * LANES:(2 * kh + 2) * LANES]], axis=0)
        if band:
            kall = jnp.concatenate([kp_ref[:, slab], kc_ref_[:, slab], kn_ref[:, slab], kx_ref[:, slab]], axis=0)
            vall = jnp.concatenate([vp_ref[:, slab], vcur_ref[:, slab], vn_ref[:, slab], vx_ref[:, slab]], axis=0)
        else:
            kall = kx_ref[:, slab]
            vall = vx_ref[:, slab]
        zero = jnp.zeros_like(kall)
        outs = []
        for half in range(2):
            khalf = jnp.where(low if half == 0 else jnp.logical_not(low), kall, zero)
            logits = lax.dot_general(q2, khalf, (((1,), (1,)), ((), ())), preferred_element_type=F32)
            if band:
                logits = jnp.concatenate([logits[:, :3 * ATT_BLOCK] + bias, logits[:, 3 * ATT_BLOCK:]], axis=1)
            h_top = 4 * kh + half
            sk = jnp.concatenate([jnp.full((ATT_BLOCK, 1), sink_ref[h_top], F32),
                                  jnp.full((ATT_BLOCK, 1), sink_ref[h_top + 2], F32)], axis=0)
            m = jnp.maximum(jnp.max(logits, axis=-1, keepdims=True), sk)
            p = jnp.exp(logits - m)
            den = jnp.sum(p, axis=-1, keepdims=True) + jnp.exp(sk - m)
            outs.append(_dot(p.astype(BF16), vall) / den)
        for j in range(2):
            r = slice(j * ATT_BLOCK, (j + 1) * ATT_BLOCK)
            o_ref[:, (2 * kh + j) * LANES:(2 * kh + j + 1) * LANES] = jnp.where(
                low_out, outs[0][r], outs[1][r]).astype(BF16)


def _attention(sink_l, q, kd, vd, kdx, vdx, *, band):
    nb, seq, _ = q.shape
    nctx = kdx.shape[1]
    nblk = seq // ATT_BLOCK
    kvw = kdx.shape[-1]
    qspec = pl.BlockSpec((None, ATT_BLOCK, D_MODEL), lambda b, n: (b, n, 0))
    xspec = pl.BlockSpec((None, nctx, kvw), lambda b, n: (b, 0, 0))
    in_specs = [pl.BlockSpec(memory_space=pltpu.SMEM), qspec]
    args = [sink_l, q]
    if band:
        prev = pl.BlockSpec((None, ATT_BLOCK, kvw), lambda b, n: (b, jnp.maximum(n - 1, 0), 0))
        cur = pl.BlockSpec((None, ATT_BLOCK, kvw), lambda b, n: (b, n, 0))
        nxt = pl.BlockSpec((None, ATT_BLOCK, kvw), lambda b, n: (b, jnp.minimum(n + 1, nblk - 1), 0))
        in_specs += [prev, cur, nxt, prev, cur, nxt]
        args += [kd, kd, kd, vd, vd, vd]
    in_specs += [xspec, xspec]
    args += [kdx, vdx]
    return pl.pallas_call(
        functools.partial(_attn_kernel, nblk=nblk, band=band),
        grid=(nb, nblk),
        in_specs=in_specs,
        out_specs=pl.BlockSpec((None, ATT_BLOCK, D_MODEL), lambda b, n: (b, n, 0)),
        out_shape=jax.ShapeDtypeStruct((nb, seq, D_MODEL), BF16),
        compiler_params=pltpu.CompilerParams(vmem_limit_bytes=_vmem_limit(40 << 20)),
        name="attention_band" if band else "attention_ctx",
    )(*args)


def _merge_mlp_kernel(x_ref, mod_ref, g1_ref, g2_ref, wg_ref, h_ref, yb_ref, yc_ref, wb_ref, wo_ref,
                      w1_ref, w2_ref, o_ref, *, mod_base, tiles_per_row):
    i = pl.program_id(0)
    row = mod_base + i // tiles_per_row

    def mod(k):
        return mod_ref[pl.ds(row, 1), k * D_MODEL:(k + 1) * D_MODEL]

    x = x_ref[...]
    h1 = (_rms(x, g1_ref[...]) * (1.0 + mod(1)) + mod(0)).astype(BF16)
    ya = (h_ref[0].astype(F32) + h_ref[1].astype(F32)).astype(BF16)
    branches = (ya, yb_ref[...], yc_ref[...])
    m = None
    for k in range(3):
        gate = jax.nn.sigmoid(_dot(h1, wg_ref[:, k * D_MODEL:(k + 1) * D_MODEL]))
        term = gate * _dot(branches[k], wb_ref[k])
        m = term if m is None else m + term
    x1 = x + mod(2) * _dot(m.astype(BF16), wo_ref[...])
    h2 = (_rms(x1, g2_ref[...]) * (1.0 + mod(4)) + mod(3)).astype(BF16)
    f = jnp.maximum(_dot(h2, w1_ref[...]), 0.0)
    o_ref[...] = x1 + mod(5) * _dot((f * f).astype(BF16), w2_ref[...])


def _merge_mlp(x2d, mod_l, g1, g2, wg, h, yb, yc, wb, wo, w1, w2, *, mod_base, tiles_per_row, tm=256):
    n_tok = x2d.shape[0]
    const = lambda i: (0, 0)
    once = pl.Buffered(1)
    tok = lambda i: (i, 0)
    return pl.pallas_call(
        functools.partial(_merge_mlp_kernel, mod_base=mod_base, tiles_per_row=tiles_per_row),
        grid=(n_tok // tm,),
        in_specs=[
            pl.BlockSpec((tm, D_MODEL), tok),
            pl.BlockSpec((MOD_ROWS, 6 * D_MODEL), const),
            pl.BlockSpec((1, D_MODEL), const),
            pl.BlockSpec((1, D_MODEL), const),
            pl.BlockSpec((D_MODEL, 3 * D_MODEL), const, pipeline_mode=once),
            pl.BlockSpec((2, tm, D_MODEL), lambda i: (0, i, 0)),
            pl.BlockSpec((tm, D_MODEL), tok),
            pl.BlockSpec((tm, D_MODEL), tok),
            pl.BlockSpec((3, D_MODEL, D_MODEL), lambda i: (0, 0, 0), pipeline_mode=once),
            pl.BlockSpec((D_MODEL, D_MODEL), const, pipeline_mode=once),
            pl.BlockSpec((D_MODEL, D_FF), const, pipeline_mode=once),
            pl.BlockSpec((D_FF, D_MODEL), const, pipeline_mode=once),
        ],
        out_specs=pl.BlockSpec((tm, D_MODEL), tok),
        out_shape=jax.ShapeDtypeStruct((n_tok, D_MODEL), F32),
        compiler_params=pltpu.CompilerParams(vmem_limit_bytes=_vmem_limit(58 << 20)),
        name="merge_mlp",
    )(x2d, mod_l, g1, g2, wg, h, yb, yc, wb, wo, w1, w2)


def _rope_tables(seq):
    pos = jnp.arange(seq)
    row = (pos // GRID_W).astype(F32)
    col = (pos % GRID_W).astype(F32)
    inv = jnp.power(ROPE_BASE, -jnp.arange(ROPE_FREQS, dtype=F32) / ROPE_FREQS)
    ang_r = row[:, None] * inv
    ang_c = col[:, None] * inv
    cos = jnp.concatenate([jnp.cos(ang_r), jnp.cos(ang_r), jnp.cos(ang_c), jnp.cos(ang_c)], axis=-1)
    sin = jnp.concatenate([-jnp.sin(ang_r), jnp.sin(ang_r), -jnp.sin(ang_c), jnp.sin(ang_c)], axis=-1)
    reps = LANES // HEAD_DIM
    return jnp.tile(cos, (1, reps)), jnp.tile(sin, (1, reps))


def _block_diag(w):
    per = MXU_DIM // RNN_BLOCK
    w5 = w.reshape(2, RNN_BLOCKS // per, per, RNN_BLOCK, RNN_BLOCK)
    eye = jnp.eye(per, dtype=w.dtype)
    bd = jnp.einsum('dcpij,pq->dcpiqj', w5, eye)
    return bd.reshape(2, RNN_BLOCKS // per, MXU_DIM, MXU_DIM).astype(BF16)


def kernel(x, c, ctx, c_ctx, w_mod, b_mod, g_norm1, w_in, conv_w, conv_b, lru_wa, lru_ba, lru_wx, lru_bx,
           lru_lambda, sgu_ln_g, sgu_ln_b, sgu_w, sgu_b, q_norm_g, k_norm_g, sink, w_branch, w_out, g_norm2,
           w_ff1, w_ff2):
    n_batch, n_tok, _ = x.shape
    n_ctx = ctx.shape[1]
    depth = w_mod.shape[0]
    assert n_batch + 1 <= MOD_ROWS and n_tok % 256 == 0 and n_ctx % 256 == 0

    cond = jnp.zeros((MOD_ROWS, D_MODEL), F32).at[:n_batch].set(c).at[n_batch].set(c_ctx)
    mod = _modulation(cond, w_mod, b_mod)

    cos, sin = _rope_tables(n_tok)
    tabs_x = (cos * ATT_SCALE, sin * ATT_SCALE, cos, sin)
    one = jnp.ones((n_ctx, LANES), F32)
    zero = jnp.zeros((n_ctx, LANES), F32)
    tabs_c = (one * ATT_SCALE, zero, one, zero)

    head = np.arange(MXU_DIM) // HEAD_DIM
    ones_bd = jnp.asarray(head[:, None] == head[None, :], BF16)

    w_in16 = w_in.astype(BF16)
    wb16 = w_branch.astype(BF16)
    wo16 = w_out.astype(BF16)
    w116 = w_ff1.astype(BF16)
    w216 = w_ff2.astype(BF16)
    ws16 = sgu_w.astype(BF16)

    x2d = x.reshape(n_batch * n_tok, D_MODEL)
    cx2d = ctx.reshape(n_batch * n_ctx, D_MODEL)
    h0 = jnp.zeros((2, n_batch, N_SLABS, LANES), F32)
    reps = LANES // HEAD_DIM

    for l in range(depth):
        last = l == depth - 1
        g1 = g_norm1[l][None]
        g2 = g_norm2[l][None]
        w_a = w_in16[l, :, :OFF_G]
        w_g = w_in16[l, :, OFF_G:]
        bs = jnp.broadcast_to(sgu_b[l][:, :, None], (SGU_GROUPS, SGU_CHUNK, LANES))
        qg = jnp.tile(q_norm_g[l], reps)[None]
        kg = jnp.tile(k_norm_g[l], reps)[None]
        sgu = (sgu_ln_g[l][None], sgu_ln_b[l][None], ws16[l], bs)
        scan_p = (conv_w[l], conv_b[l][None], _block_diag(lru_wa[l]), _block_diag(lru_wx[l]),
                  lru_ba[l][:, None], lru_bx[l][:, None], lru_lambda[l][:, None])

        xa_c, yb_c, q_c, kd_c, vd_c = _inproj(cx2d, mod[l], g1, w_a, *sgu, qg, kg, ones_bd, tabs_c,
                                              seq=n_ctx, mod_base=n_batch, per_batch=False)
        h_c, hfin_c = _bidir_scan(xa_c.reshape(n_batch, n_ctx, D_MODEL), *scan_p, h0)
        kd_c = kd_c.reshape(n_batch, n_ctx, -1)
        vd_c = vd_c.reshape(n_batch, n_ctx, -1)

        xa, yb, q, kd, vd = _inproj(x2d, mod[l], g1, w_a, *sgu, qg, kg, ones_bd, tabs_x,
                                    seq=n_tok, mod_base=0, per_batch=True)
        h_x, _ = _bidir_scan(xa.reshape(n_batch, n_tok, D_MODEL), *scan_p, hfin_c)
        yc = _attention(sink[l], q.reshape(n_batch, n_tok, D_MODEL), kd.reshape(n_batch, n_tok, -1),
                        vd.reshape(n_batch, n_tok, -1), kd_c, vd_c, band=True)
        x2d = _merge_mlp(x2d, mod[l], g1, g2, w_g, h_x.reshape(2, n_batch * n_tok, D_MODEL), yb,
                         yc.reshape(n_batch * n_tok, D_MODEL), wb16[l], wo16[l], w116[l], w216[l],
                         mod_base=0, tiles_per_row=n_tok // 256)

        if not last:
            yc_c = _attention(sink[l], q_c.reshape(n_batch, n_ctx, D_MODEL), None, None, kd_c, vd_c, band=False)
            cx2d = _merge_mlp(cx2d, mod[l], g1, g2, w_g, h_c.reshape(2, n_batch * n_ctx, D_MODEL), yb_c,
                              yc_c.reshape(n_batch * n_ctx, D_MODEL), wb16[l], wo16[l], w116[l], w216[l],
                              mod_base=n_batch, tiles_per_row=n_batch * n_ctx // 256)

    return x2d.reshape(n_batch, n_tok, D_MODEL)
```

```python
import functools

import numpy as np
import jax
import jax.numpy as jnp
from jax import lax
from jax.experimental import pallas as pl
from jax.experimental.pallas import tpu as pltpu

F32 = jnp.float32
BF16 = jnp.bfloat16

D_MODEL = 1024
GRID_W = 64
EPS = 1e-6
RNN_BLOCKS = 16
RNN_BLOCK = D_MODEL // RNN_BLOCKS
CONV_W = 4
CONV_PAD_L = 2
LRU_C = 8.0
SGU_CHUNK = 128
SGU_GROUPS = 8
N_HEADS = 16
N_KV_HEADS = 4
HEAD_DIM = 64
WINDOW = 128
ATT_BLOCK = 128
ATT_SCALE = HEAD_DIM ** -0.5
ROPE_BASE = 10000.0
ROPE_FREQS = HEAD_DIM // 4
D_FF = 4 * D_MODEL
OFF_B = D_MODEL
OFF_Q = OFF_B + 2 * D_MODEL
OFF_K = OFF_Q + N_HEADS * HEAD_DIM
OFF_V = OFF_K + N_KV_HEADS * HEAD_DIM
OFF_G = OFF_V + N_KV_HEADS * HEAD_DIM

LANES = 128
SUBLANES = 8
MXU_DIM = 256
VMEM_BYTES = 64 * 1024 * 1024

N_SLABS = D_MODEL // LANES
MOD_ROWS = SUBLANES
NEG_BIG = -1e30
LOG2E = 1.4426950408889634
LN2 = 0.6931471805599453
SCAN_GROUP = 16


def _vmem_limit(nbytes):
    return int(min(nbytes, VMEM_BYTES - 6 * 1024 * 1024))


def _rms(x, g):
    ms = jnp.mean(x * x, axis=-1, keepdims=True)
    return x * lax.rsqrt(ms + EPS) * g


def _gelu_tanh(x):
    c = np.sqrt(2.0 / np.pi).astype(np.float32)
    return x * (0.5 * (1.0 + jnp.tanh(c * (x + 0.044715 * (x * x * x)))))


def _dot(a, b):
    return jnp.dot(a, b, preferred_element_type=F32)


def _mod_kernel(c_ref, w_ref, b_ref, o_ref):
    c = c_ref[...]
    s = c * jax.nn.sigmoid(c)
    o_ref[0] = _dot(s.astype(BF16), w_ref[0].astype(BF16)) + b_ref[0]


def _modulation(cond, w_mod, b_mod):
    depth, _, width = w_mod.shape
    tn = 1536
    return pl.pallas_call(
        _mod_kernel,
        grid=(depth, width // tn),
        in_specs=[
            pl.BlockSpec((MOD_ROWS, D_MODEL), lambda l, j: (0, 0)),
            pl.BlockSpec((1, D_MODEL, tn), lambda l, j: (l, 0, j)),
            pl.BlockSpec((1, 1, tn), lambda l, j: (l, 0, j)),
        ],
        out_specs=pl.BlockSpec((1, MOD_ROWS, tn), lambda l, j: (l, 0, j)),
        out_shape=jax.ShapeDtypeStruct((depth, MOD_ROWS, width), F32),
        compiler_params=pltpu.CompilerParams(vmem_limit_bytes=_vmem_limit(40 << 20)),
        name="modulation",
    )(cond, w_mod, b_mod.reshape(depth, 1, width))


def _inproj_kernel(x_ref, mod_ref, g1_ref, w_ref, lng_ref, lnb_ref, ws_ref, bs_ref, qg_ref, kg_ref,
                   ones_ref, cq_ref, sq_ref, ck_ref, sk_ref,
                   xa_ref, yb_ref, q_ref, kd_ref, vd_ref, *, tm, mod_base, tiles_per_row):
    i = pl.program_id(0)
    row = mod_base + i // tiles_per_row
    shift = mod_ref[pl.ds(row, 1), 0:D_MODEL]
    scale = mod_ref[pl.ds(row, 1), D_MODEL:2 * D_MODEL]
    h = (_rms(x_ref[...], g1_ref[...]) * (1.0 + scale) + shift).astype(BF16)

    q_raw = _dot(h, w_ref[:, OFF_Q:OFF_K])
    k_raw = _dot(h, w_ref[:, OFF_K:OFF_V])
    vv = _dot(h, w_ref[:, OFF_V:OFF_G])

    lane = lax.broadcasted_iota(jnp.int32, (tm, LANES), 1)
    first_half = (lane % (2 * ROPE_FREQS)) < ROPE_FREQS
    low_head = lane < HEAD_DIM

    def head_norm_rope(z, g_ref, c_ref, s_ref, out_ref):
        zz = (z * z).astype(BF16)
        for blk in range(z.shape[1] // MXU_DIM):
            cols = slice(blk * MXU_DIM, (blk + 1) * MXU_DIM)
            ms = _dot(zz[:, cols], ones_ref[...]) * (1.0 / HEAD_DIM)
            zn = z[:, cols] * lax.rsqrt(ms + EPS)
            for s in range(MXU_DIM // LANES):
                t = zn[:, s * LANES:(s + 1) * LANES] * g_ref[...]
                sw = jnp.where(first_half, pltpu.roll(t, LANES - ROPE_FREQS, 1),
                               pltpu.roll(t, ROPE_FREQS, 1))
                slab = blk * (MXU_DIM // LANES) + s
                out_ref(slab, t * c_ref[...] + sw * s_ref[...])

    def store_q(slab, val):
        q_ref[:, slab * LANES:(slab + 1) * LANES] = val.astype(BF16)

    def dup_heads(slab_val):
        r = pltpu.roll(slab_val, HEAD_DIM, 1)
        return jnp.where(low_head, slab_val, r), jnp.where(low_head, r, slab_val)

    def store_kd(slab, val):
        a, b = dup_heads(val)
        kd_ref[:, (2 * slab) * LANES:(2 * slab + 1) * LANES] = a.astype(BF16)
        kd_ref[:, (2 * slab + 1) * LANES:(2 * slab + 2) * LANES] = b.astype(BF16)

    head_norm_rope(q_raw, qg_ref, cq_ref, sq_ref, store_q)
    head_norm_rope(k_raw, kg_ref, ck_ref, sk_ref, store_kd)

    for s in range((OFF_G - OFF_V) // LANES):
        a, b = dup_heads(vv[:, s * LANES:(s + 1) * LANES])
        vd_ref[:, (2 * s) * LANES:(2 * s + 1) * LANES] = a.astype(BF16)
        vd_ref[:, (2 * s + 1) * LANES:(2 * s + 2) * LANES] = b.astype(BF16)

    v = _gelu_tanh(_dot(h, w_ref[:, OFF_B + D_MODEL:OFF_Q]))
    u = _gelu_tanh(_dot(h, w_ref[:, OFF_B:OFF_B + D_MODEL]))
    mu = jnp.mean(v, axis=-1, keepdims=True)
    vc = v - mu
    var = jnp.mean(vc * vc, axis=-1, keepdims=True)
    vn = (vc * lax.rsqrt(var + EPS) * lng_ref[...] + lnb_ref[...]).astype(BF16)
    n_chunks = tm // SGU_CHUNK
    for g in range(SGU_GROUPS):
        cols = slice(g * LANES, (g + 1) * LANES)
        rhs = jnp.concatenate([vn[c * SGU_CHUNK:(c + 1) * SGU_CHUNK, cols] for c in range(n_chunks)], axis=1)
        mixed = _dot(ws_ref[g], rhs)
        for c in range(n_chunks):
            rows = slice(c * SGU_CHUNK, (c + 1) * SGU_CHUNK)
            yb_ref[rows, cols] = (u[rows, cols] * (mixed[:, c * LANES:(c + 1) * LANES] + bs_ref[g])).astype(BF16)

    xa_ref[...] = _dot(h, w_ref[:, 0:OFF_B])


def _inproj(layer, x2d, mod, g1, w_a, lng, lnb, ws, bs, qg, kg, ones_bd, tabs, *, seq, mod_base, per_batch, tm):
    n_tok = x2d.shape[0]
    tiles_per_seq = seq // tm
    tiles_per_row = tiles_per_seq if per_batch else n_tok // tm
    cq, sq, ck, sk = tabs
    const = lambda i: (0, 0)
    lay2 = lambda i: (layer, 0, 0)
    tab_map = lambda i: (i % tiles_per_seq, 0)
    kvw = 2 * N_KV_HEADS * HEAD_DIM
    return pl.pallas_call(
        functools.partial(_inproj_kernel, tm=tm, mod_base=mod_base, tiles_per_row=tiles_per_row),
        grid=(n_tok // tm,),
        in_specs=[
            pl.BlockSpec((tm, D_MODEL), lambda i: (i, 0)),
            pl.BlockSpec((None, MOD_ROWS, 6 * D_MODEL), lay2),
            pl.BlockSpec((1, D_MODEL), const),
            pl.BlockSpec((None, D_MODEL, OFF_G), lay2, pipeline_mode=pl.Buffered(1)),
            pl.BlockSpec((1, D_MODEL), const),
            pl.BlockSpec((1, D_MODEL), const),
            pl.BlockSpec((None, SGU_GROUPS, SGU_CHUNK, SGU_CHUNK), lambda i: (layer, 0, 0, 0)),
            pl.BlockSpec((SGU_GROUPS, SGU_CHUNK, LANES), lambda i: (0, 0, 0)),
            pl.BlockSpec((1, LANES), const),
            pl.BlockSpec((1, LANES), const),
            pl.BlockSpec((MXU_DIM, MXU_DIM), const),
            pl.BlockSpec((tm, LANES), tab_map),
            pl.BlockSpec((tm, LANES), tab_map),
            pl.BlockSpec((tm, LANES), tab_map),
            pl.BlockSpec((tm, LANES), tab_map),
        ],
        out_specs=[
            pl.BlockSpec((tm, D_MODEL), lambda i: (i, 0)),
            pl.BlockSpec((tm, D_MODEL), lambda i: (i, 0)),
            pl.BlockSpec((tm, D_MODEL), lambda i: (i, 0)),
            pl.BlockSpec((tm, kvw), lambda i: (i, 0)),
            pl.BlockSpec((tm, kvw), lambda i: (i, 0)),
        ],
        out_shape=[
            jax.ShapeDtypeStruct((n_tok, D_MODEL), F32),
            jax.ShapeDtypeStruct((n_tok, D_MODEL), BF16),
            jax.ShapeDtypeStruct((n_tok, D_MODEL), BF16),
            jax.ShapeDtypeStruct((n_tok, kvw), BF16),
            jax.ShapeDtypeStruct((n_tok, kvw), BF16),
        ],
        compiler_params=pltpu.CompilerParams(vmem_limit_bytes=_vmem_limit(56 << 20)),
        name="inproj",
    )(x2d, mod, g1, w_a, lng, lnb, ws, bs, qg, kg, ones_bd, cq, sq, ck, sk)


def _scan_kernel(*refs, tc, nchunk, nb, reverse, conv):
    if conv:
        (xa_ref, xp_ref, xn_ref, cw_ref, cb_ref, wa_ref, wx_ref, ba_ref, bx_ref, lam_ref, h0_ref,
         out_ref, hfin_ref, xc_ref, xe_ref, a_ref, u_ref, h_ref, hc_ref) = refs
    else:
        (xc_ref, wa_ref, wx_ref, ba_ref, bx_ref, lam_ref, h0_ref,
         out_ref, hfin_ref, a_ref, u_ref, h_ref, hc_ref) = refs
    i = pl.program_id(0)
    c = (nchunk - 1 - i) if reverse else i

    @pl.when(i == 0)
    def _():
        hc_ref[...] = h0_ref[...]

    z = -lam_ref[...]
    softplus = jnp.maximum(z, 0.0) + jnp.log1p(jnp.exp(-jnp.abs(z)))
    nc2l = (-0.5 * LRU_C * LOG2E) * softplus

    for b in range(nb):
        if conv:
            xe_ref[0:SUBLANES, :] = xp_ref[b] * (c > 0).astype(F32)
            xe_ref[SUBLANES:SUBLANES + tc, :] = xa_ref[b]
            xe_ref[SUBLANES + tc:2 * SUBLANES + tc, :] = xn_ref[b] * (c < nchunk - 1).astype(F32)
            xe = xe_ref[...]
            n_rows = tc + 2 * SUBLANES
            xc = cb_ref[...]
            for k in range(CONV_W):
                back = CONV_PAD_L - k
                xs = xe if back == 0 else pltpu.roll(xe, back % n_rows, 0)
                xc = xc + cw_ref[k:k + 1, :] * xs[SUBLANES:SUBLANES + tc]
            xc_ref[b] = xc
        else:
            xc = xc_ref[b]
        for blk in range(D_MODEL // MXU_DIM):
            cols = slice(blk * MXU_DIM, (blk + 1) * MXU_DIM)
            xcb = xc[:, cols]
            xcb16 = xcb.astype(BF16)
            t_r = jnp.tanh(_dot(xcb16, wa_ref[blk]) + ba_ref[:, cols])
            t_i = jnp.tanh(_dot(xcb16, wx_ref[blk]) + bx_ref[:, cols])
            log2_a = nc2l[:, cols] * t_r + nc2l[:, cols]
            a = jnp.exp2(log2_a)
            s = jnp.tanh(log2_a * (-LN2)) * (a * a + 1.0)
            root = jnp.where(s > 0.0, s * lax.rsqrt(s), 0.0)
            u = xcb * (t_i + 1.0) * root
            for sl in range(MXU_DIM // LANES):
                slab = blk * (MXU_DIM // LANES) + sl
                a_ref[b, pl.ds(slab, tc, stride=N_SLABS), :] = a[:, sl * LANES:(sl + 1) * LANES]
                u_ref[b, pl.ds(slab, tc, stride=N_SLABS), :] = u[:, sl * LANES:(sl + 1) * LANES]

    n_groups = tc // SCAN_GROUP

    def group(g, hs):
        gg = (n_groups - 1 - g) if reverse else g
        base = pl.multiple_of(gg * (SCAN_GROUP * N_SLABS), SCAN_GROUP * N_SLABS)
        hs = list(hs)
        for j in range(SCAN_GROUP):
            rows = pl.ds(base + ((SCAN_GROUP - 1 - j) if reverse else j) * N_SLABS, N_SLABS)
            for b in range(nb):
                hs[b] = a_ref[b, rows, :] * hs[b] + u_ref[b, rows, :]
                h_ref[b, rows, :] = hs[b]
        return tuple(hs)

    hs = lax.fori_loop(0, n_groups, group, tuple(hc_ref[b] for b in range(nb)))
    for b in range(nb):
        hc_ref[b] = hs[b]
        hfin_ref[b] = hs[b]
        for slab in range(N_SLABS):
            out_ref[b, :, slab * LANES:(slab + 1) * LANES] = (
                h_ref[b, pl.ds(slab, tc, stride=N_SLABS), :].astype(BF16))


def _scan(layer, direction, x_in, cw, cb, wa_bd, wx_bd, ba, bx, lam, h0):
    nb, seq, _ = x_in.shape
    tc = 256
    nchunk = seq // tc
    conv = direction == 0
    reverse = direction == 1
    chunk = (lambda i: nchunk - 1 - i) if reverse else (lambda i: i)
    nblk = D_MODEL // MXU_DIM
    tok_spec = pl.BlockSpec((nb, tc, D_MODEL), lambda i: (0, chunk(i), 0))
    gate_w = pl.BlockSpec((None, None, nblk, MXU_DIM, MXU_DIM), lambda i: (layer, direction, 0, 0, 0))
    vec = pl.BlockSpec((None, None, 1, D_MODEL), lambda i: (layer, direction, 0, 0))
    state = pl.BlockSpec((None, nb, N_SLABS, LANES), lambda i: (direction, 0, 0, 0))
    in_specs = [tok_spec]
    args = [x_in]
    if conv:
        halo_blocks = seq // SUBLANES
        per = tc // SUBLANES
        in_specs += [
            pl.BlockSpec((nb, SUBLANES, D_MODEL), lambda i: (0, jnp.maximum(i * per - 1, 0), 0)),
            pl.BlockSpec((nb, SUBLANES, D_MODEL), lambda i: (0, jnp.minimum((i + 1) * per, halo_blocks - 1), 0)),
            pl.BlockSpec((None, CONV_W, D_MODEL), lambda i: (layer, 0, 0)),
            pl.BlockSpec((None, 1, D_MODEL), lambda i: (layer, 0, 0)),
        ]
        args += [x_in, x_in, cw, cb]
    in_specs += [gate_w, gate_w, vec, vec, vec, state]
    args += [wa_bd, wx_bd, ba, bx, lam, h0]
    out_specs = [
        pl.BlockSpec((nb, tc, D_MODEL), lambda i: (0, chunk(i), 0)),
        pl.BlockSpec((nb, N_SLABS, LANES), lambda i: (0, 0, 0)),
    ]
    out_shape = [
        jax.ShapeDtypeStruct((nb, seq, D_MODEL), BF16),
        jax.ShapeDtypeStruct((nb, N_SLABS, LANES), F32),
    ]
    scratch = []
    if conv:
        out_specs.append(pl.BlockSpec((nb, tc, D_MODEL), lambda i: (0, i, 0)))
        out_shape.append(jax.ShapeDtypeStruct((nb, seq, D_MODEL), F32))
        scratch.append(pltpu.VMEM((tc + 2 * SUBLANES, D_MODEL), F32))
    scratch += [pltpu.VMEM((nb, tc * N_SLABS, LANES), F32)] * 3 + [pltpu.VMEM((nb, N_SLABS, LANES), F32)]
    return pl.pallas_call(
        functools.partial(_scan_kernel, tc=tc, nchunk=nchunk, nb=nb, reverse=reverse, conv=conv),
        grid=(nchunk,),
        in_specs=in_specs,
        out_specs=out_specs,
        out_shape=out_shape,
        scratch_shapes=scratch,
        compiler_params=pltpu.CompilerParams(vmem_limit_bytes=_vmem_limit(56 << 20)),
        name="scan_fwd" if conv else "scan_bwd",
    )(*args)


def _attn_kernel(sink_ref, q_ref, *refs, nblk, band):
    if band:
        kp_ref, kc_ref_, kn_ref, vp_ref, vcur_ref, vn_ref, kx_ref, vx_ref, o_ref = refs
    else:
        kx_ref, vx_ref, o_ref = refs
    n = pl.program_id(1)
    lane =lax.broadcasted_iota(jnp.int32, (1, LANES), 1)
    low = lane < HEAD_DIM
    high = jnp.logical_not(low)

    if band:
        qi = lax.broadcasted_iota(jnp.int32, (ATT_BLOCK, ATT_BLOCK), 0)
        kj = lax.broadcasted_iota(jnp.int32, (ATT_BLOCK, ATT_BLOCK), 1)
        bp = jnp.where((kj >= qi) & (n > 0), 0.0, NEG_BIG).astype(F32)
        bn = jnp.where((kj <= qi) & (n < nblk - 1), 0.0, NEG_BIG).astype(F32)
        bias_prev = jnp.concatenate([bp, bp], axis=0)
        bias_next = jnp.concatenate([bn, bn], axis=0)

    def operands(kh):
        slab = slice(kh * LANES, (kh + 1) * LANES)
        q2 = jnp.concatenate([q_ref[:, (2 * kh) * LANES:(2 * kh + 1) * LANES],
                              q_ref[:, (2 * kh + 1) * LANES:(2 * kh + 2) * LANES]], axis=0)
        if band:
            kall = jnp.concatenate([kp_ref[:, slab], kc_ref_[:, slab], kn_ref[:, slab], kx_ref[:, slab]], axis=0)
            vall = jnp.concatenate([vp_ref[:, slab], vcur_ref[:, slab], vn_ref[:, slab], vx_ref[:, slab]], axis=0)
        else:
            kall = kx_ref[:, slab]
            vall = vx_ref[:, slab]
        return q2, kall, vall

    def scores(chain):
        kh, half = divmod(chain, 2)
        q2, kall, _ = operands(kh)
        khalf = jnp.where(low if half == 0 else high, kall, jnp.zeros_like(kall))
        return lax.dot_general(q2, khalf, (((1,), (1,)), ((), ())), preferred_element_type=F32)

    def weights(chain, logits):
        kh, half = divmod(chain, 2)
        pieces = [logits[:, j * LANES:(j + 1) * LANES] for j in range(logits.shape[1] // LANES)]
        if band:
            pieces[0] = pieces[0] + bias_prev
            pieces[2] = pieces[2] + bias_next
        h_top = 4 * kh + half
        sk = jnp.concatenate([jnp.full((ATT_BLOCK, 1), sink_ref[h_top] * LOG2E, F32),
                              jnp.full((ATT_BLOCK, 1), sink_ref[h_top + 2] * LOG2E, F32)], axis=0)
        widest = pieces[0]
        for piece in pieces[1:]:
            widest = jnp.maximum(widest, piece)
        m = jnp.maximum(sk, jnp.max(widest, axis=-1, keepdims=True))
        p = jnp.concatenate([jnp.exp2(piece - m).astype(BF16) for piece in pieces], axis=1)
        return p, jnp.exp2(sk - m)

    def values(chain, p):
        kh, half = divmod(chain, 2)
        _, _, vall = operands(kh)
        vhalf = jnp.where(low if half == 0 else high, vall, jnp.ones_like(vall))
        return _dot(p, vhalf)

    def finish(kh, pv, sink_p):
        num = jnp.where(low, pv[0], pv[1])
        den = pltpu.roll(jnp.where(low, pv[1], pv[0]), HEAD_DIM, 1) + jnp.where(low, sink_p[0], sink_p[1])
        res = num / den
        for j in range(2):
            o_ref[:, (2 * kh + j) * LANES:(2 * kh + j + 1) * LANES] = (
                res[j * ATT_BLOCK:(j + 1) * ATT_BLOCK].astype(BF16))

    n_chains = 2 * N_KV_HEADS
    logits = {0: scores(0)}
    probs, sink_p, pv = {}, {}, {}
    for step in range(n_chains + 1):
        if step + 1 < n_chains:
            logits[step + 1] = scores(step + 1)
        if step < n_chains:
            probs[step], sink_p[step] = weights(step, logits.pop(step))
        c = step - 1
        if c >= 0:
            pv[c] = values(c, probs.pop(c))
            if c % 2 == 1:
                finish(c // 2, [pv.pop(c - 1), pv.pop(c)], [sink_p.pop(c - 1), sink_p.pop(c)])


def _attention(sink_l, q, kd, vd, kdx, vdx, *, band):
    nb, seq, _ = q.shape
    nctx = kdx.shape[1]
    nblk = seq // ATT_BLOCK
    kvw = kdx.shape[-1]
    qspec = pl.BlockSpec((None, ATT_BLOCK, D_MODEL), lambda b, n: (b, n, 0))
    xspec = pl.BlockSpec((None, nctx, kvw), lambda b, n: (b, 0, 0))
    in_specs = [pl.BlockSpec(memory_space=pltpu.SMEM), qspec]
    args = [sink_l, q]
    if band:
        prev = pl.BlockSpec((None, ATT_BLOCK, kvw), lambda b, n: (b, jnp.maximum(n - 1, 0), 0))
        cur = pl.BlockSpec((None, ATT_BLOCK, kvw), lambda b, n: (b, n, 0))
        nxt = pl.BlockSpec((None, ATT_BLOCK, kvw), lambda b, n: (b, jnp.minimum(n + 1, nblk - 1), 0))
        in_specs += [prev, cur, nxt, prev, cur, nxt]
        args += [kd, kd, kd, vd, vd, vd]
    in_specs += [xspec, xspec]
    args += [kdx, vdx]
    return pl.pallas_call(
        functools.partial(_attn_kernel, nblk=nblk, band=band),
        grid=(nb, nblk),
        in_specs=in_specs,
        out_specs=pl.BlockSpec((None, ATT_BLOCK, D_MODEL), lambda b, n: (b, n, 0)),
        out_shape=jax.ShapeDtypeStruct((nb, seq, D_MODEL), BF16),
        compiler_params=pltpu.CompilerParams(vmem_limit_bytes=_vmem_limit(40 << 20)),
        name="attention_band" if band else "attention_ctx",
    )(*args)


def _merge_mlp_kernel(x_ref, mod_ref, g1_ref, g2_ref, wg_ref, hf_ref, hb_ref, yb_ref, yc_ref, wb_ref, wo_ref,
                      w1_ref, w2_ref, o_ref, *, mod_base, tiles_per_row):
    i = pl.program_id(0)
    row = mod_base + i // tiles_per_row

    def mod(k):
        return mod_ref[pl.ds(row, 1), k * D_MODEL:(k + 1) * D_MODEL]

    x = x_ref[...]
    h1 = (_rms(x, g1_ref[...]) * (1.0 + mod(1)) + mod(0)).astype(BF16)
    ya = (hf_ref[...].astype(F32) + hb_ref[...].astype(F32)).astype(BF16)
    branches = (ya, yb_ref[...], yc_ref[...])
    m = None
    for k in range(3):
        gate = jax.nn.sigmoid(_dot(h1, wg_ref[:, k * D_MODEL:(k + 1) * D_MODEL]))
        term = gate * _dot(branches[k], wb_ref[k])
        m = term if m is None else m + term
    x1 = x + mod(2) * _dot(m.astype(BF16), wo_ref[...])
    h2 = (_rms(x1, g2_ref[...]) * (1.0 + mod(4)) + mod(3)).astype(BF16)
    f = jnp.maximum(_dot(h2, w1_ref[...]), 0.0)
    o_ref[...] = x1 + mod(5) * _dot((f * f).astype(BF16), w2_ref[...])


def _merge_mlp(layer, x2d, mod, g1, g2, wg, hf, hb, yb, yc, wb, wo, w1, w2, *, mod_base, tiles_per_row, tm=256):
    n_tok = x2d.shape[0]
    const = lambda i: (0, 0)
    lay2 = lambda i: (layer, 0, 0)
    once = pl.Buffered(1)
    tok = lambda i: (i, 0)
    return pl.pallas_call(
        functools.partial(_merge_mlp_kernel, mod_base=mod_base, tiles_per_row=tiles_per_row),
        grid=(n_tok // tm,),
        in_specs=[
            pl.BlockSpec((tm, D_MODEL), tok),
            pl.BlockSpec((None, MOD_ROWS, 6 * D_MODEL), lay2),
            pl.BlockSpec((1, D_MODEL), const),
            pl.BlockSpec((1, D_MODEL), const),
            pl.BlockSpec((None, D_MODEL, 3 * D_MODEL), lay2, pipeline_mode=once),
            pl.BlockSpec((tm, D_MODEL), tok),
            pl.BlockSpec((tm, D_MODEL), tok),
            pl.BlockSpec((tm, D_MODEL), tok),
            pl.BlockSpec((tm, D_MODEL), tok),
            pl.BlockSpec((None, 3, D_MODEL, D_MODEL), lambda i: (layer, 0, 0, 0), pipeline_mode=once),
            pl.BlockSpec((None, D_MODEL, D_MODEL), lay2, pipeline_mode=once),
            pl.BlockSpec((None, D_MODEL, D_FF), lay2, pipeline_mode=once),
            pl.BlockSpec((None, D_FF, D_MODEL), lay2, pipeline_mode=once),
        ],
        out_specs=pl.BlockSpec((tm, D_MODEL), tok),
        out_shape=jax.ShapeDtypeStruct((n_tok, D_MODEL), F32),
        compiler_params=pltpu.CompilerParams(vmem_limit_bytes=_vmem_limit(58 << 20)),
        name="merge_mlp",
    )(x2d, mod, g1, g2, wg, hf, hb, yb, yc, wb, wo, w1, w2)


def _rope_tables(seq):
    pos = jnp.arange(seq)
    row = (pos // GRID_W).astype(F32)
    col = (pos % GRID_W).astype(F32)
    inv = jnp.power(ROPE_BASE, -jnp.arange(ROPE_FREQS, dtype=F32) / ROPE_FREQS)
    ang_r = row[:, None] * inv
    ang_c = col[:, None] * inv
    cos = jnp.concatenate([jnp.cos(ang_r), jnp.cos(ang_r), jnp.cos(ang_c), jnp.cos(ang_c)], axis=-1)
    sin = jnp.concatenate([-jnp.sin(ang_r), jnp.sin(ang_r), -jnp.sin(ang_c), jnp.sin(ang_c)], axis=-1)
    reps = LANES // HEAD_DIM
    return jnp.tile(cos, (1, reps)), jnp.tile(sin, (1, reps))


def _block_diag(w):
    depth = w.shape[0]
    per = MXU_DIM // RNN_BLOCK
    w6 = w.reshape(depth, 2, RNN_BLOCKS // per, per, RNN_BLOCK, RNN_BLOCK)
    eye = jnp.eye(per, dtype=w.dtype)
    bd = jnp.einsum('ldcpij,pq->ldcpiqj', w6, eye)
    return bd.reshape(depth, 2, RNN_BLOCKS // per, MXU_DIM, MXU_DIM).astype(BF16)


def kernel(x, c, ctx, c_ctx, w_mod, b_mod, g_norm1, w_in, conv_w, conv_b, lru_wa, lru_ba, lru_wx, lru_bx,
           lru_lambda, sgu_ln_g, sgu_ln_b, sgu_w, sgu_b, q_norm_g, k_norm_g, sink, w_branch, w_out, g_norm2,
           w_ff1, w_ff2):
    n_batch, n_tok, _ = x.shape
    n_ctx = ctx.shape[1]
    depth = w_mod.shape[0]
    tm_x, tm_c = 512, 256
    assert n_batch + 1 <= MOD_ROWS and n_tok % tm_x == 0 and n_ctx % tm_c == 0

    cond = jnp.zeros((MOD_ROWS, D_MODEL), F32).at[:n_batch].set(c).at[n_batch].set(c_ctx)
    mod = _modulation(cond, w_mod, b_mod)

    cos, sin = _rope_tables(n_tok)
    q_mul = ATT_SCALE * LOG2E
    tabs_x = (cos * q_mul, sin * q_mul, cos, sin)
    one = jnp.ones((n_ctx, LANES), F32)
    zero = jnp.zeros((n_ctx, LANES), F32)
    tabs_c = (one * q_mul, zero, one, zero)

    head = np.arange(MXU_DIM) // HEAD_DIM
    ones_bd = jnp.asarray(head[:, None] == head[None, :], BF16)

    w_a16 = w_in[:, :, :OFF_G].astype(BF16)
    w_g16 = w_in[:, :, OFF_G:].astype(BF16)
    wb16 = w_branch.astype(BF16)
    wo16 = w_out.astype(BF16)
    w116 = w_ff1.astype(BF16)
    w216 = w_ff2.astype(BF16)
    ws16 = sgu_w.astype(BF16)
    wa_bd = _block_diag(lru_wa)
    wx_bd = _block_diag(lru_wx)
    ba_h = (0.5 * lru_ba)[:, :, None]
    bx_h = (0.5 * lru_bx)[:, :, None]
    lam = lru_lambda[:, :, None]
    conv_wh = 0.5 * conv_w
    conv_bh = (0.5 * conv_b)[:, None]

    x2d = x.reshape(n_batch * n_tok, D_MODEL)
    cx2d = ctx.reshape(n_batch * n_ctx, D_MODEL)
    h0 = jnp.zeros((2, n_batch, N_SLABS, LANES), F32)
    reps = LANES // HEAD_DIM

    def rglru(layer, xa, init):
        scan_p = (wa_bd, wx_bd, ba_h, bx_h, lam)
        hf, hf_fin, xc = _scan(layer, 0, xa, conv_wh, conv_bh, *scan_p, init)
        hb, hb_fin = _scan(layer, 1, xc, None, None, *scan_p, init)
        return hf, hb, jnp.stack([hf_fin, hb_fin])

    for l in range(depth):
        last = l == depth - 1
        g1 = g_norm1[l][None]
        g2 = g_norm2[l][None]
        bs = jnp.broadcast_to(sgu_b[l][:, :, None], (SGU_GROUPS, SGU_CHUNK, LANES))
        qg = jnp.tile(q_norm_g[l], reps)[None]
        kg = jnp.tile(k_norm_g[l], reps)[None]
        inproj_p = (mod, g1, w_a16, sgu_ln_g[l][None], sgu_ln_b[l][None], ws16, bs, qg, kg, ones_bd)
        merge_w = (wb16, wo16, w116, w216)

        xa_c, yb_c, q_c, kd_c, vd_c = _inproj(l, cx2d, *inproj_p, tabs_c, seq=n_ctx, mod_base=n_batch,
                                              per_batch=False, tm=tm_c)
        hf_c, hb_c, hfin_c = rglru(l, xa_c.reshape(n_batch, n_ctx, D_MODEL), h0)
        kd_c = kd_c.reshape(n_batch, n_ctx, -1)
        vd_c = vd_c.reshape(n_batch, n_ctx, -1)

        xa, yb, q, kd, vd = _inproj(l, x2d, *inproj_p, tabs_x, seq=n_tok, mod_base=0, per_batch=True, tm=tm_x)
        hf_x, hb_x, _ = rglru(l, xa.reshape(n_batch, n_tok, D_MODEL), hfin_c)
        yc = _attention(sink[l], q.reshape(n_batch, n_tok, D_MODEL), kd.reshape(n_batch, n_tok, -1),
                        vd.reshape(n_batch, n_tok, -1), kd_c, vd_c, band=True)
        x2d = _merge_mlp(l, x2d, mod, g1, g2, w_g16, hf_x.reshape(n_batch * n_tok, D_MODEL),
                         hb_x.reshape(n_batch * n_tok, D_MODEL), yb, yc.reshape(n_batch * n_tok, D_MODEL),
                         *merge_w, mod_base=0, tiles_per_row=n_tok // 256)

        if not last:
            yc_c = _attention(sink[l], q_c.reshape(n_batch, n_ctx, D_MODEL), None, None, kd_c, vd_c, band=False)
            cx2d = _merge_mlp(l, cx2d, mod, g1, g2, w_g16, hf_c.reshape(n_batch * n_ctx, D_MODEL),
                              hb_c.reshape(n_batch * n_ctx, D_MODEL), yb_c,
                              yc_c.reshape(n_batch * n_ctx, D_MODEL), *merge_w,
                              mod_base=n_batch, tiles_per_row=n_batch * n_ctx // 256)

    return x2d.reshape(n_batch, n_tok, D_MODEL)
```

```python
import functools

import numpy as np
import jax
import jax.numpy as jnp
from jax import lax
from jax.experimental import pallas as pl
from jax.experimental.pallas import tpu as pltpu

F32 = jnp.float32
BF16 = jnp.bfloat16

D_MODEL = 1024
GRID_W = 64
EPS = 1e-6
RNN_BLOCKS = 16
RNN_BLOCK = D_MODEL // RNN_BLOCKS
CONV_W = 4
CONV_PAD_L = 2
LRU_C = 8.0
SGU_CHUNK = 128
SGU_GROUPS = 8
N_HEADS = 16
N_KV_HEADS = 4
HEAD_DIM = 64
WINDOW = 128
ATT_BLOCK = 128
ATT_SCALE = HEAD_DIM ** -0.5
ROPE_BASE = 10000.0
ROPE_FREQS = HEAD_DIM // 4
D_FF = 4 * D_MODEL
OFF_B = D_MODEL
OFF_Q = OFF_B + 2 * D_MODEL
OFF_K = OFF_Q + N_HEADS * HEAD_DIM
OFF_V = OFF_K + N_KV_HEADS * HEAD_DIM
OFF_G = OFF_V + N_KV_HEADS * HEAD_DIM

LANES = 128
SUBLANES = 8
MXU_DIM = 256
VMEM_BYTES = 64 * 1024 * 1024

N_SLABS = D_MODEL // LANES
MOD_ROWS = SUBLANES
NEG_BIG = -1e30
LOG2E = 1.4426950408889634
LN2 = 0.6931471805599453
SCAN_GROUP = 16
GATE_BLK = 1536
MERGE_SUB = 256
FF_CHUNKS = 2
ATT_QB = 4


def _vmem_limit(nbytes):
    return int(min(nbytes, VMEM_BYTES - 4 * 1024 * 1024))


def _rms(x, g):
    ms = jnp.mean(x * x, axis=-1, keepdims=True)
    return x * lax.rsqrt(ms + EPS) * g


def _gelu_tanh(x):
    c = np.sqrt(2.0 / np.pi).astype(np.float32)
    return x * (0.5 * (1.0 + jnp.tanh(c * (x + 0.044715 * (x * x * x)))))


def _dot(a, b):
    return jnp.dot(a, b, preferred_element_type=F32)


def _mod_kernel(c_ref, w_ref, b_ref, o_ref):
    c = c_ref[...]
    s = c * jax.nn.sigmoid(c)
    o_ref[0] = _dot(s.astype(BF16), w_ref[0].astype(BF16)) + b_ref[0]


def _modulation(cond, w_mod, b_mod):
    depth, _, width = w_mod.shape
    tn = 1536
    return pl.pallas_call(
        _mod_kernel,
        grid=(depth, width // tn),
        in_specs=[
            pl.BlockSpec((MOD_ROWS, D_MODEL), lambda l, j: (0, 0)),
            pl.BlockSpec((1, D_MODEL, tn), lambda l, j: (l, 0, j)),
            pl.BlockSpec((1, 1, tn), lambda l, j: (l, 0, j)),
        ],
        out_specs=pl.BlockSpec((1, MOD_ROWS, tn), lambda l, j: (l, 0, j)),
        out_shape=jax.ShapeDtypeStruct((depth, MOD_ROWS, width), F32),
        compiler_params=pltpu.CompilerParams(vmem_limit_bytes=_vmem_limit(40 << 20)),
        name="modulation",
    )(cond, w_mod, b_mod.reshape(depth, 1, width))


def _inproj_kernel(x_ref, mod_ref, g1_ref, w_ref, lng_ref, lnb_ref, ws_ref, bs_ref, qg_ref, kg_ref,
                   ones_ref, cq_ref, sq_ref, ck_ref, sk_ref,
                   xa_ref, yb_ref, q_ref, kd_ref, vd_ref, *, tm, mod_base, tiles_per_row):
    i = pl.program_id(0)
    row = mod_base + i // tiles_per_row
    shift = mod_ref[pl.ds(row, 1), 0:D_MODEL]
    scale = mod_ref[pl.ds(row, 1), D_MODEL:2 * D_MODEL]
    h = (_rms(x_ref[...], g1_ref[...]) * (1.0 + scale) + shift).astype(BF16)

    q_raw = _dot(h, w_ref[:, OFF_Q:OFF_K])
    k_raw = _dot(h, w_ref[:, OFF_K:OFF_V])
    vv = _dot(h, w_ref[:, OFF_V:OFF_G])

    lane = lax.broadcasted_iota(jnp.int32, (tm, LANES), 1)
    first_half = (lane % (2 * ROPE_FREQS)) < ROPE_FREQS
    low_head = lane < HEAD_DIM

    def head_norm_rope(z, g_ref, c_ref, s_ref, out_ref):
        zz = (z * z).astype(BF16)
        for blk in range(z.shape[1] // MXU_DIM):
            cols = slice(blk * MXU_DIM, (blk + 1) * MXU_DIM)
            ms = _dot(zz[:, cols], ones_ref[...]) * (1.0 / HEAD_DIM)
            zn = z[:, cols] * lax.rsqrt(ms + EPS)
            for s in range(MXU_DIM // LANES):
                t = zn[:, s * LANES:(s + 1) * LANES] * g_ref[...]
                sw = jnp.where(first_half, pltpu.roll(t, LANES - ROPE_FREQS, 1),
                               pltpu.roll(t, ROPE_FREQS, 1))
                slab = blk * (MXU_DIM // LANES) + s
                out_ref(slab, t * c_ref[...] + sw * s_ref[...])

    def store_q(slab, val):
        q_ref[:, slab * LANES:(slab + 1) * LANES] = val.astype(BF16)

    def dup_heads(slab_val):
        r = pltpu.roll(slab_val, HEAD_DIM, 1)
        return jnp.where(low_head, slab_val, r), jnp.where(low_head, r, slab_val)

    def store_kd(slab, val):
        a, b = dup_heads(val)
        kd_ref[:, (2 * slab) * LANES:(2 * slab + 1) * LANES] = a.astype(BF16)
        kd_ref[:, (2 * slab + 1) * LANES:(2 * slab + 2) * LANES] = b.astype(BF16)

    head_norm_rope(q_raw, qg_ref, cq_ref, sq_ref, store_q)
    head_norm_rope(k_raw, kg_ref, ck_ref, sk_ref, store_kd)

    for s in range((OFF_G - OFF_V) // LANES):
        a, b = dup_heads(vv[:, s * LANES:(s + 1) * LANES])
        vd_ref[:, (2 * s) * LANES:(2 * s + 1) * LANES] = a.astype(BF16)
        vd_ref[:, (2 * s + 1) * LANES:(2 * s + 2) * LANES] = b.astype(BF16)

    v = _gelu_tanh(_dot(h, w_ref[:, OFF_B + D_MODEL:OFF_Q]))
    u = _gelu_tanh(_dot(h, w_ref[:, OFF_B:OFF_B + D_MODEL]))
    mu = jnp.mean(v, axis=-1, keepdims=True)
    vc = v - mu
    var = jnp.mean(vc * vc, axis=-1, keepdims=True)
    vn = (vc * lax.rsqrt(var + EPS) * lng_ref[...] + lnb_ref[...]).astype(BF16)
    n_chunks = tm // SGU_CHUNK
    for g in range(SGU_GROUPS):
        cols = slice(g * LANES, (g + 1) * LANES)
        rhs = jnp.concatenate([vn[c * SGU_CHUNK:(c + 1) * SGU_CHUNK, cols] for c in range(n_chunks)], axis=1)
        mixed = _dot(ws_ref[g], rhs)
        for c in range(n_chunks):
            rows = slice(c * SGU_CHUNK, (c + 1) * SGU_CHUNK)
            yb_ref[rows, cols] = (u[rows, cols] * (mixed[:, c * LANES:(c + 1) * LANES] + bs_ref[g])).astype(BF16)

    xa_ref[...] = _dot(h, w_ref[:, 0:OFF_B])


def _inproj(layer, x2d, mod, g1, w_a, lng, lnb, ws, bs, qg, kg, ones_bd, tabs, *, seq, mod_base, per_batch, tm):
    n_tok = x2d.shape[0]
    tiles_per_seq = seq // tm
    tiles_per_row = tiles_per_seq if per_batch else n_tok // tm
    cq, sq, ck, sk = tabs
    const = lambda i: (0, 0)
    lay2 = lambda i: (layer, 0, 0)
    tab_map = lambda i: (i % tiles_per_seq, 0)
    kvw = 2 * N_KV_HEADS * HEAD_DIM
    return pl.pallas_call(
        functools.partial(_inproj_kernel, tm=tm, mod_base=mod_base, tiles_per_row=tiles_per_row),
        grid=(n_tok // tm,),
        in_specs=[
            pl.BlockSpec((tm, D_MODEL), lambda i: (i, 0)),
            pl.BlockSpec((None, MOD_ROWS, 6 * D_MODEL), lay2),
            pl.BlockSpec((1, D_MODEL), const),
            pl.BlockSpec((None, D_MODEL, OFF_G), lay2, pipeline_mode=pl.Buffered(1)),
            pl.BlockSpec((1, D_MODEL), const),
            pl.BlockSpec((1, D_MODEL), const),
            pl.BlockSpec((None, SGU_GROUPS, SGU_CHUNK, SGU_CHUNK), lambda i: (layer, 0, 0, 0)),
            pl.BlockSpec((SGU_GROUPS, SGU_CHUNK, LANES), lambda i: (0, 0, 0)),
            pl.BlockSpec((1, LANES), const),
            pl.BlockSpec((1, LANES), const),
            pl.BlockSpec((MXU_DIM, MXU_DIM), const),
            pl.BlockSpec((tm, LANES), tab_map),
            pl.BlockSpec((tm, LANES), tab_map),
            pl.BlockSpec((tm, LANES), tab_map),
            pl.BlockSpec((tm, LANES), tab_map),
        ],
        out_specs=[
            pl.BlockSpec((tm, D_MODEL), lambda i: (i, 0)),
            pl.BlockSpec((tm, D_MODEL), lambda i: (i, 0)),
            pl.BlockSpec((tm, D_MODEL), lambda i: (i, 0)),
            pl.BlockSpec((tm, kvw), lambda i: (i, 0)),
            pl.BlockSpec((tm, kvw), lambda i: (i, 0)),
        ],
        out_shape=[
            jax.ShapeDtypeStruct((n_tok, D_MODEL), F32),
            jax.ShapeDtypeStruct((n_tok, D_MODEL), BF16),
            jax.ShapeDtypeStruct((n_tok, D_MODEL), BF16),
            jax.ShapeDtypeStruct((n_tok, kvw), BF16),
            jax.ShapeDtypeStruct((n_tok, kvw), BF16),
        ],
        compiler_params=pltpu.CompilerParams(vmem_limit_bytes=_vmem_limit(56 << 20)),
        name="inproj",
    )(x2d, mod, g1, w_a, lng, lnb, ws, bs, qg, kg, ones_bd, cq, sq, ck, sk)


def _scan_kernel(*refs, tc, nchunk, nb, reverse, conv):
    if conv:
        (xa_ref, xp_ref, xn_ref, cw_ref, cb_ref, wa_ref, wx_ref, ba_ref, bx_ref, lam_ref, h0_ref,
         out_ref, hfin_ref, xc_ref, xe_ref, a_ref, u_ref, h_ref, hc_ref) = refs
    else:
        (xc_ref, hf_ref, wa_ref, wx_ref, ba_ref, bx_ref, lam_ref, h0_ref,
         out_ref, hfin_ref, a_ref, u_ref, h_ref, hc_ref) = refs
    i = pl.program_id(0)
    c = (nchunk - 1 - i) if reverse else i

    @pl.when(i == 0)
    def _():
        hc_ref[...] = h0_ref[...]

    z = -lam_ref[...]
    softplus = jnp.maximum(z, 0.0) + jnp.log1p(jnp.exp(-jnp.abs(z)))
    nc2l = (-0.5 * LRU_C * LOG2E) * softplus

    for b in range(nb):
        if conv:
            xe_ref[0:SUBLANES, :] = xp_ref[b] * (c > 0).astype(F32)
            xe_ref[SUBLANES:SUBLANES + tc, :] = xa_ref[b]
            xe_ref[SUBLANES + tc:2 * SUBLANES + tc, :] = xn_ref[b] * (c < nchunk - 1).astype(F32)
            xe = xe_ref[...]
            n_rows = tc + 2 * SUBLANES
            xc = cb_ref[...]
            for k in range(CONV_W):
                back = CONV_PAD_L - k
                xs = xe if back == 0 else pltpu.roll(xe, back % n_rows, 0)
                xc = xc + cw_ref[k:k + 1, :] * xs[SUBLANES:SUBLANES + tc]
            xc_ref[b] = xc
        else:
            xc = xc_ref[b]
        for blk in range(D_MODEL // MXU_DIM):
            cols = slice(blk * MXU_DIM, (blk + 1) * MXU_DIM)
            xcb = xc[:, cols]
            xcb16 = xcb.astype(BF16)
            t_r = jnp.tanh(_dot(xcb16, wa_ref[blk]) + ba_ref[:, cols])
            t_i = jnp.tanh(_dot(xcb16, wx_ref[blk]) + bx_ref[:, cols])
            log2_a = nc2l[:, cols] * t_r + nc2l[:, cols]
            a = jnp.exp2(log2_a)
            s = jnp.tanh(log2_a * (-LN2)) * (a * a + 1.0)
            root = jnp.where(s > 0.0, s * lax.rsqrt(s), 0.0)
            u = xcb * (t_i + 1.0) * root
            for sl in range(MXU_DIM // LANES):
                slab = blk * (MXU_DIM // LANES) + sl
                a_ref[b, pl.ds(slab, tc, stride=N_SLABS), :] = a[:, sl * LANES:(sl + 1) * LANES]
                u_ref[b, pl.ds(slab, tc, stride=N_SLABS), :] = u[:, sl * LANES:(sl + 1) * LANES]

    n_groups = tc // SCAN_GROUP

    def group(g, hs):
        gg = (n_groups - 1 - g) if reverse else g
        base = pl.multiple_of(gg * (SCAN_GROUP * N_SLABS), SCAN_GROUP * N_SLABS)
        hs = list(hs)
        for j in range(SCAN_GROUP):
            rows = pl.ds(base + ((SCAN_GROUP - 1 - j) if reverse else j) * N_SLABS, N_SLABS)
            for b in range(nb):
                hs[b] = a_ref[b, rows, :] * hs[b] + u_ref[b, rows, :]
                h_ref[b, rows, :] = hs[b]
        return tuple(hs)

    hs = lax.fori_loop(0, n_groups, group, tuple(hc_ref[b] for b in range(nb)))
    for b in range(nb):
        hc_ref[b] = hs[b]
        hfin_ref[b] = hs[b]
        for slab in range(N_SLABS):
            cols = slice(slab * LANES, (slab + 1) * LANES)
            h_tok = h_ref[b, pl.ds(slab, tc, stride=N_SLABS), :]
            if not conv:
                h_tok = h_tok + hf_ref[b, :, cols].astype(F32)
            out_ref[b, :, cols] = h_tok.astype(BF16)


def _scan(layer, direction, x_in, hf, cw, cb, wa_bd, wx_bd, ba, bx, lam, h0):
    nb, seq, _ = x_in.shape
    tc = 256
    nchunk = seq // tc
    conv = direction == 0
    reverse = direction == 1
    chunk = (lambda i: nchunk - 1 - i) if reverse else (lambda i: i)
    nblk = D_MODEL // MXU_DIM
    tok_spec = pl.BlockSpec((nb, tc, D_MODEL), lambda i: (0, chunk(i), 0))
    gate_w = pl.BlockSpec((None, None, nblk, MXU_DIM, MXU_DIM), lambda i: (layer, direction, 0, 0, 0))
    vec = pl.BlockSpec((None, None, 1, D_MODEL), lambda i: (layer, direction, 0, 0))
    state = pl.BlockSpec((None, nb, N_SLABS, LANES), lambda i: (direction, 0, 0, 0))
    in_specs = [tok_spec]
    args = [x_in]
    if conv:
        halo_blocks = seq // SUBLANES
        per = tc // SUBLANES
        in_specs += [
            pl.BlockSpec((nb, SUBLANES, D_MODEL), lambda i: (0, jnp.maximum(i * per - 1, 0), 0)),
            pl.BlockSpec((nb, SUBLANES, D_MODEL), lambda i: (0, jnp.minimum((i + 1) * per, halo_blocks - 1), 0)),
            pl.BlockSpec((None, CONV_W, D_MODEL), lambda i: (layer, 0, 0)),
            pl.BlockSpec((None, 1, D_MODEL), lambda i: (layer, 0, 0)),
        ]
        args += [x_in, x_in, cw, cb]
    else:
        in_specs.append(tok_spec)
        args.append(hf)
    in_specs += [gate_w, gate_w, vec, vec, vec, state]
    args += [wa_bd, wx_bd, ba, bx, lam, h0]
    out_specs = [
        pl.BlockSpec((nb, tc, D_MODEL), lambda i: (0, chunk(i), 0)),
        pl.BlockSpec((nb, N_SLABS, LANES), lambda i: (0, 0, 0)),
    ]
    out_shape = [
        jax.ShapeDtypeStruct((nb, seq, D_MODEL), BF16),
        jax.ShapeDtypeStruct((nb, N_SLABS, LANES), F32),
    ]
    scratch = []
    if conv:
        out_specs.append(pl.BlockSpec((nb, tc, D_MODEL), lambda i: (0, i, 0)))
        out_shape.append(jax.ShapeDtypeStruct((nb, seq, D_MODEL), F32))
        scratch.append(pltpu.VMEM((tc + 2 * SUBLANES, D_MODEL), F32))
    scratch += [pltpu.VMEM((nb, tc * N_SLABS, LANES), F32)] * 3 + [pltpu.VMEM((nb, N_SLABS, LANES), F32)]
    return pl.pallas_call(
        functools.partial(_scan_kernel, tc=tc, nchunk=nchunk, nb=nb, reverse=reverse, conv=conv),
        grid=(nchunk,),
        in_specs=in_specs,
        out_specs=out_specs,
        out_shape=out_shape,
        scratch_shapes=scratch,
        compiler_params=pltpu.CompilerParams(vmem_limit_bytes=_vmem_limit(56 << 20)),
        name="scan_fwd" if conv else "scan_bwd",
    )(*args)


def _attn_kernel(sink_ref, q_ref, *refs, n_steps, qb, band):
    if band:
        kp_ref, kcur_ref, kn_ref, vp_ref, vcur_ref, vn_ref, kx_ref, vx_ref, o_ref = refs
    else:
        kx_ref, vx_ref, o_ref = refs
    n = pl.program_id(1)
    lane = lax.broadcasted_iota(jnp.int32, (1, LANES), 1)
    low = lane < HEAD_DIM
    high = jnp.logical_not(low)

    def block_rows(j):
        return slice(j * ATT_BLOCK, (j + 1) * ATT_BLOCK)

    if band:
        qi = lax.broadcasted_iota(jnp.int32, (ATT_BLOCK, ATT_BLOCK), 0)
        kj = lax.broadcasted_iota(jnp.int32, (ATT_BLOCK, ATT_BLOCK), 1)

        def tiled(visible):
            one = jnp.where(visible, 0.0, NEG_BIG).astype(F32)
            return jnp.concatenate([one, one], axis=0)

        bias_prev = [tiled((kj >= qi) & (n > 0)) if j == 0 else tiled(kj >= qi) for j in range(qb)]
        bias_next = [tiled((kj <= qi) & (n < n_steps - 1)) if j == qb - 1 else tiled(kj <= qi) for j in range(qb)]

    def split(chain):
        j, rest = divmod(chain, 2 * N_KV_HEADS)
        kh, half = divmod(rest, 2)
        return j, kh, half

    def keys_values(j, kh, refs3, ctx_ref):
        slab = slice(kh * LANES, (kh + 1) * LANES)
        if not band:
            return ctx_ref[:, slab]
        before, cur, after = refs3
        blocks = [before[:, slab]] + [cur[block_rows(i), slab] for i in range(qb)] + [after[:, slab]]
        return jnp.concatenate(blocks[j:j + 3] + [ctx_ref[:, slab]], axis=0)

    def scores(chain):
        j, kh, half = split(chain)
        q2 = jnp.concatenate([q_ref[block_rows(j), (2 * kh) * LANES:(2 * kh + 1) * LANES],
                              q_ref[block_rows(j), (2 * kh + 1) * LANES:(2 * kh + 2) * LANES]], axis=0)
        kall = keys_values(j, kh, (kp_ref, kcur_ref, kn_ref) if band else None, kx_ref)
        khalf = jnp.where(low if half == 0 else high, kall, jnp.zeros_like(kall))
        return lax.dot_general(q2, khalf, (((1,), (1,)), ((), ())), preferred_element_type=F32)

    def weights(chain, logits):
        j, kh, half = split(chain)
        pieces = [logits[:, i * LANES:(i + 1) * LANES] for i in range(logits.shape[1] // LANES)]
        if band:
            pieces[0] = pieces[0] + bias_prev[j]
            pieces[2] = pieces[2] + bias_next[j]
        h_top = 4 * kh + half
        sk = jnp.concatenate([jnp.full((ATT_BLOCK, 1), sink_ref[h_top] * LOG2E, F32),
                              jnp.full((ATT_BLOCK, 1), sink_ref[h_top + 2] * LOG2E, F32)], axis=0)
        widest = pieces[0]
        for piece in pieces[1:]:
            widest = jnp.maximum(widest, piece)
        m = jnp.maximum(sk, jnp.max(widest, axis=-1, keepdims=True))
        p = jnp.concatenate([jnp.exp2(piece - m).astype(BF16) for piece in pieces], axis=1)
        return p, jnp.exp2(sk - m)

    def values(chain, p):
        j, kh, half = split(chain)
        vall = keys_values(j, kh, (vp_ref, vcur_ref, vn_ref) if band else None, vx_ref)
        vhalf = jnp.where(low if half == 0 else high, vall, jnp.ones_like(vall))
        return _dot(p, vhalf)

    def finish(j, kh, pv, sink_p):
        num = jnp.where(low, pv[0], pv[1])
        den = pltpu.roll(jnp.where(low, pv[1], pv[0]), HEAD_DIM, 1) + jnp.where(low, sink_p[0], sink_p[1])
        res = num / den
        for i in range(2):
            o_ref[block_rows(j), (2 * kh + i) * LANES:(2 * kh + i + 1) * LANES] = (
                res[i * ATT_BLOCK:(i + 1) * ATT_BLOCK].astype(BF16))

    n_chains = qb * 2 * N_KV_HEADS
    logits = {0: scores(0)}
    probs, sink_p, pv = {}, {}, {}
    for step in range(n_chains + 1):
        if step + 1 < n_chains:
            logits[step + 1] = scores(step + 1)
        if step < n_chains:
            probs[step], sink_p[step] = weights(step, logits.pop(step))
        c = step - 1
        if c >= 0:
            pv[c] = values(c, probs.pop(c))
            if c % 2 == 1:
                j, kh, _ = split(c)
                finish(j, kh, [pv.pop(c - 1), pv.pop(c)], [sink_p.pop(c - 1), sink_p.pop(c)])


def _attention(sink_l, q, kd, vd, kdx, vdx, *, band, qb=ATT_QB):
    nb, seq, _ = q.shape
    nctx = kdx.shape[1]
    nblk = seq // ATT_BLOCK
    qb = min(qb, nblk)
    assert nblk % qb == 0
    n_steps = nblk // qb
    kvw = kdx.shape[-1]
    qspec = pl.BlockSpec((None, qb * ATT_BLOCK, D_MODEL), lambda b, n: (b, n, 0))
    xspec = pl.BlockSpec((None, nctx, kvw), lambda b, n: (b, 0, 0))
    in_specs = [pl.BlockSpec(memory_space=pltpu.SMEM), qspec]
    args = [sink_l, q]
    if band:
        prev = pl.BlockSpec((None, ATT_BLOCK, kvw), lambda b, n: (b, jnp.maximum(n * qb - 1, 0), 0))
        cur = pl.BlockSpec((None, qb * ATT_BLOCK, kvw), lambda b, n: (b, n, 0))
        nxt = pl.BlockSpec((None, ATT_BLOCK, kvw), lambda b, n: (b, jnp.minimum((n + 1) * qb, nblk - 1), 0))
        in_specs += [prev, cur, nxt, prev, cur, nxt]
        args += [kd, kd, kd, vd, vd, vd]
    in_specs += [xspec, xspec]
    args += [kdx, vdx]
    return pl.pallas_call(
        functools.partial(_attn_kernel, n_steps=n_steps, qb=qb, band=band),
        grid=(nb, n_steps),
        in_specs=in_specs,
        out_specs=pl.BlockSpec((None, qb * ATT_BLOCK, D_MODEL), lambda b, n: (b, n, 0)),
        out_shape=jax.ShapeDtypeStruct((nb, seq, D_MODEL), BF16),
        compiler_params=pltpu.CompilerParams(vmem_limit_bytes=_vmem_limit(40 << 20)),
        name="attention_band" if band else "attention_ctx",
    )(*args)


def _merge_mlp_kernel(x_ref, mod_ref, g1_ref, g2_ref, wga_ref, wgb_ref, ya_ref, yb_ref, yc_ref, wb_ref, wo_ref,
                      w1_ref, w2_ref, o_ref, *, mod_base, tiles_per_row):
    i = pl.program_id(0)
    row = mod_base + i // tiles_per_row

    def mod(k):
        return mod_ref[pl.ds(row, 1), k * D_MODEL:(k + 1) * D_MODEL]

    def gate_logits(h1, k):
        parts = []
        for ref, base in ((wga_ref, 0), (wgb_ref, GATE_BLK)):
            lo = max(k * D_MODEL, base) - base
            hi = min((k + 1) * D_MODEL, base + GATE_BLK) - base
            if hi > lo:
                parts.append(_dot(h1, ref[:, lo:hi]))
        return parts[0] if len(parts) == 1 else jnp.concatenate(parts, axis=1)

    subs = [slice(j * MERGE_SUB, (j + 1) * MERGE_SUB) for j in range(x_ref.shape[0] // MERGE_SUB)]
    xs = [x_ref[r, :] for r in subs]
    ms = []
    for j, r in enumerate(subs):
        h1 = (_rms(xs[j], g1_ref[...]) * (1.0 + mod(1)) + mod(0)).astype(BF16)
        branches = (ya_ref[r, :], yb_ref[r, :], yc_ref[r, :])
        m = None
        for k in range(3):
            gate = jax.nn.sigmoid(gate_logits(h1, k))
            term = gate * _dot(branches[k], wb_ref[k])
            m = term if m is None else m + term
        ms.append(m.astype(BF16))
    x1s = [xs[j] + mod(2) * _dot(ms[j], wo_ref[...]) for j in range(len(subs))]
    h2s = [(_rms(x1, g2_ref[...]) * (1.0 + mod(4)) + mod(3)).astype(BF16) for x1 in x1s]
    acc = list(x1s)
    ff = D_FF // FF_CHUNKS
    for c in range(FF_CHUNKS):
        fs = [jnp.maximum(_dot(h2, w1_ref[:, c * ff:(c + 1) * ff]), 0.0) for h2 in h2s]
        for j in range(len(subs)):
            acc[j] = acc[j] + mod(5) * _dot((fs[j] * fs[j]).astype(BF16), w2_ref[c * ff:(c + 1) * ff, :])
    for j, r in enumerate(subs):
        o_ref[r, :] = acc[j]


def _merge_mlp(layer, x2d, mod, g1, g2, wg, ya, yb, yc, wb, wo, w1, w2, *, mod_base, tiles_per_row, tm):
    n_tok = x2d.shape[0]
    const = lambda i: (0, 0)
    lay2 = lambda i: (layer, 0, 0)
    once = pl.Buffered(1)
    tok = lambda i: (i, 0)
    return pl.pallas_call(
        functools.partial(_merge_mlp_kernel, mod_base=mod_base, tiles_per_row=tiles_per_row),
        grid=(n_tok // tm,),
        in_specs=[
            pl.BlockSpec((tm, D_MODEL), tok),
            pl.BlockSpec((None, MOD_ROWS, 6 * D_MODEL), lay2),
            pl.BlockSpec((1, D_MODEL), const),
            pl.BlockSpec((1, D_MODEL), const),
            pl.BlockSpec((None, D_MODEL, GATE_BLK), lambda i: (layer, 0, OFF_G // GATE_BLK), pipeline_mode=once),
            pl.BlockSpec((None, D_MODEL, GATE_BLK), lambda i: (layer, 0, OFF_G // GATE_BLK + 1), pipeline_mode=once),
            pl.BlockSpec((tm, D_MODEL), tok),
            pl.BlockSpec((tm, D_MODEL), tok),
            pl.BlockSpec((tm, D_MODEL), tok),
            pl.BlockSpec((None, 3, D_MODEL, D_MODEL), lambda i: (layer, 0, 0, 0), pipeline_mode=once),
            pl.BlockSpec((None, D_MODEL, D_MODEL), lay2, pipeline_mode=once),
            pl.BlockSpec((None, D_MODEL, D_FF), lay2, pipeline_mode=once),
            pl.BlockSpec((None, D_FF, D_MODEL), lay2, pipeline_mode=once),
        ],
        out_specs=pl.BlockSpec((tm, D_MODEL), tok),
        out_shape=jax.ShapeDtypeStruct((n_tok, D_MODEL), F32),
        compiler_params=pltpu.CompilerParams(vmem_limit_bytes=_vmem_limit(60 << 20)),
        name="merge_mlp",
    )(x2d, mod, g1, g2, wg, wg, ya, yb, yc, wb, wo, w1, w2)


def _rope_tables(seq):
    pos = jnp.arange(seq)
    row = (pos // GRID_W).astype(F32)
    col = (pos % GRID_W).astype(F32)
    inv = jnp.power(ROPE_BASE, -jnp.arange(ROPE_FREQS, dtype=F32) / ROPE_FREQS)
    ang_r = row[:, None] * inv
    ang_c = col[:, None] * inv
    cos = jnp.concatenate([jnp.cos(ang_r), jnp.cos(ang_r), jnp.cos(ang_c), jnp.cos(ang_c)], axis=-1)
    sin = jnp.concatenate([-jnp.sin(ang_r), jnp.sin(ang_r), -jnp.sin(ang_c), jnp.sin(ang_c)], axis=-1)
    reps = LANES // HEAD_DIM
    return jnp.tile(cos, (1, reps)), jnp.tile(sin, (1, reps))


def _block_diag(w):
    depth = w.shape[0]
    per = MXU_DIM // RNN_BLOCK
    w6 = w.reshape(depth, 2, RNN_BLOCKS // per, per, RNN_BLOCK, RNN_BLOCK)
    eye = jnp.eye(per, dtype=w.dtype)
    bd = jnp.einsum('ldcpij,pq->ldcpiqj', w6, eye)
    return bd.reshape(depth, 2, RNN_BLOCKS // per, MXU_DIM, MXU_DIM).astype(BF16)


def kernel(x, c, ctx, c_ctx, w_mod, b_mod, g_norm1, w_in, conv_w, conv_b, lru_wa, lru_ba, lru_wx, lru_bx,
           lru_lambda, sgu_ln_g, sgu_ln_b, sgu_w, sgu_b, q_norm_g, k_norm_g, sink, w_branch, w_out, g_norm2,
           w_ff1, w_ff2):
    n_batch, n_tok, _ = x.shape
    n_ctx = ctx.shape[1]
    depth = w_mod.shape[0]
    tm_x, tm_c = 512, 256
    assert n_batch + 1 <= MOD_ROWS and n_tok % tm_x == 0 and n_ctx % tm_c == 0

    cond = jnp.zeros((MOD_ROWS, D_MODEL), F32).at[:n_batch].set(c).at[n_batch].set(c_ctx)
    mod = _modulation(cond, w_mod, b_mod)

    cos, sin = _rope_tables(n_tok)
    q_mul = ATT_SCALE * LOG2E
    tabs_x = (cos * q_mul, sin * q_mul, cos, sin)
    one = jnp.ones((n_ctx, LANES), F32)
    zero = jnp.zeros((n_ctx, LANES), F32)
    tabs_c = (one * q_mul, zero, one, zero)

    head = np.arange(MXU_DIM) // HEAD_DIM
    ones_bd = jnp.asarray(head[:, None] == head[None, :], BF16)

    w_in16 = w_in.astype(BF16)
    wb16 = w_branch.astype(BF16)
    wo16 = w_out.astype(BF16)
    w116 = w_ff1.astype(BF16)
    w216 = w_ff2.astype(BF16)
    ws16 = sgu_w.astype(BF16)
    wa_bd = _block_diag(lru_wa)
    wx_bd = _block_diag(lru_wx)
    ba_h = (0.5 * lru_ba)[:, :, None]
    bx_h = (0.5 * lru_bx)[:, :, None]
    lam = lru_lambda[:, :, None]
    conv_wh = 0.5 * conv_w
    conv_bh = (0.5 * conv_b)[:, None]

    x2d = x.reshape(n_batch * n_tok, D_MODEL)
    cx2d = ctx.reshape(n_batch * n_ctx, D_MODEL)
    h0 = jnp.zeros((2, n_batch, N_SLABS, LANES), F32)
    reps = LANES // HEAD_DIM

    def rglru(layer, xa, init):
        scan_p = (wa_bd, wx_bd, ba_h, bx_h, lam)
        hf, hf_fin, xc = _scan(layer, 0, xa, None, conv_wh, conv_bh, *scan_p, init)
        ya, hb_fin = _scan(layer, 1, xc, hf, None, None, *scan_p, init)
        return ya, jnp.stack([hf_fin, hb_fin])

    for l in range(depth):
        last = l == depth - 1
        g1 = g_norm1[l][None]
        g2 = g_norm2[l][None]
        bs = jnp.broadcast_to(sgu_b[l][:, :, None], (SGU_GROUPS, SGU_CHUNK, LANES))
        qg = jnp.tile(q_norm_g[l], reps)[None]
        kg = jnp.tile(k_norm_g[l], reps)[None]
        inproj_p = (mod, g1, w_in16, sgu_ln_g[l][None], sgu_ln_b[l][None], ws16, bs, qg, kg, ones_bd)
        merge_w = (wb16, wo16, w116, w216)

        xa_c, yb_c, q_c, kd_c, vd_c = _inproj(l, cx2d, *inproj_p, tabs_c, seq=n_ctx, mod_base=n_batch,
                                              per_batch=False, tm=tm_c)
        ya_c, hfin_c = rglru(l, xa_c.reshape(n_batch, n_ctx, D_MODEL), h0)
        kd_c = kd_c.reshape(n_batch, n_ctx, -1)
        vd_c = vd_c.reshape(n_batch, n_ctx, -1)

        xa, yb, q, kd, vd = _inproj(l, x2d, *inproj_p, tabs_x, seq=n_tok, mod_base=0, per_batch=True, tm=tm_x)
        ya_x, _ = rglru(l, xa.reshape(n_batch, n_tok, D_MODEL), hfin_c)
        yc = _attention(sink[l], q.reshape(n_batch, n_tok, D_MODEL), kd.reshape(n_batch, n_tok, -1),
                        vd.reshape(n_batch, n_tok, -1), kd_c, vd_c, band=True)
        x2d = _merge_mlp(l, x2d, mod, g1, g2, w_in16, ya_x.reshape(n_batch * n_tok, D_MODEL), yb,
                         yc.reshape(n_batch * n_tok, D_MODEL), *merge_w,
                         mod_base=0, tiles_per_row=n_tok // tm_x, tm=tm_x)

        if not last:
            yc_c = _attention(sink[l], q_c.reshape(n_batch, n_ctx, D_MODEL), None, None, kd_c, vd_c, band=False)
            cx2d = _merge_mlp(l, cx2d, mod, g1, g2, w_in16, ya_c.reshape(n_batch * n_ctx, D_MODEL), yb_c,
                              yc_c.reshape(n_batch * n_ctx, D_MODEL), *merge_w,
                              mod_base=n_batch, tiles_per_row=n_batch * n_ctx // tm_c, tm=tm_c)

    return x2d.reshape(n_batch, n_tok, D_MODEL)
```

```python
import functools

import numpy as np
import jax
import jax.numpy as jnp
from jax import lax
from jax.experimental import pallas as pl
from jax.experimental.pallas import tpu as pltpu

F32 = jnp.float32
BF16 = jnp.bfloat16

D_MODEL = 1024
GRID_W = 64
EPS = 1e-6
RNN_BLOCKS = 16
RNN_BLOCK = D_MODEL // RNN_BLOCKS
CONV_W = 4
CONV_PAD_L = 2
LRU_C = 8.0
SGU_CHUNK = 128
SGU_GROUPS = 8
N_HEADS = 16
N_KV_HEADS = 4
HEAD_DIM = 64
WINDOW = 128
ATT_BLOCK = 128
ATT_SCALE = HEAD_DIM ** -0.5
ROPE_BASE = 10000.0
ROPE_FREQS = HEAD_DIM // 4
D_FF = 4 * D_MODEL
OFF_B = D_MODEL
OFF_Q = OFF_B + 2 * D_MODEL
OFF_K = OFF_Q + N_HEADS * HEAD_DIM
OFF_V = OFF_K + N_KV_HEADS * HEAD_DIM
OFF_G = OFF_V + N_KV_HEADS * HEAD_DIM

LANES = 128
SUBLANES = 8
MXU_DIM = 256
VMEM_BYTES = 64 * 1024 * 1024

N_SLABS = D_MODEL // LANES
MOD_ROWS = SUBLANES
NEG_BIG = -1e30
LOG2E = 1.4426950408889634
LN2 = 0.6931471805599453
SCAN_GROUP = 16
GATE_BLK = 1536
MERGE_SUB = 256
FF_CHUNKS = 2
ATT_QB = 8


def _vmem_limit(nbytes):
    return int(min(nbytes, VMEM_BYTES - 4 * 1024 * 1024))


def _rms(x, g):
    ms = jnp.mean(x * x, axis=-1, keepdims=True)
    return x * lax.rsqrt(ms + EPS) * g


def _gelu_tanh(x):
    c = np.sqrt(2.0 / np.pi).astype(np.float32)
    inner = x * (c + (0.044715 * c) * (x * x))
    hx = 0.5 * x
    return hx + hx * jnp.tanh(inner)


def _dot(a, b):
    return jnp.dot(a, b, preferred_element_type=F32)


def _mod_kernel(c_ref, w_ref, b_ref, o_ref):
    c = c_ref[...]
    s = c * jax.nn.sigmoid(c)
    o_ref[0] = _dot(s.astype(BF16), w_ref[0].astype(BF16)) + b_ref[0]


def _modulation(cond, w_mod, b_mod):
    depth, _, width = w_mod.shape
    tn = 1536
    return pl.pallas_call(
        _mod_kernel,
        grid=(depth, width // tn),
        in_specs=[
            pl.BlockSpec((MOD_ROWS, D_MODEL), lambda l, j: (0, 0)),
            pl.BlockSpec((1, D_MODEL, tn), lambda l, j: (l, 0, j)),
            pl.BlockSpec((1, 1, tn), lambda l, j: (l, 0, j)),
        ],
        out_specs=pl.BlockSpec((1, MOD_ROWS, tn), lambda l, j: (l, 0, j)),
        out_shape=jax.ShapeDtypeStruct((depth, MOD_ROWS, width), F32),
        compiler_params=pltpu.CompilerParams(vmem_limit_bytes=_vmem_limit(40 << 20)),
        name="modulation",
    )(cond, w_mod, b_mod.reshape(depth, 1, width))


def _inproj_kernel(x_ref, mod_ref, g1_ref, w_ref, lng_ref, lnb_ref, ws_ref, bs_ref, qg_ref, kg_ref,
                   ones_ref, cq_ref, sq_ref, ck_ref, sk_ref,
                   xa_ref, yb_ref, q_ref, kd_ref, vd_ref, *, tm, mod_base, tiles_per_row):
    i = pl.program_id(0)
    row = mod_base + i // tiles_per_row
    shift = mod_ref[pl.ds(row, 1), 0:D_MODEL]
    scale = mod_ref[pl.ds(row, 1), D_MODEL:2 * D_MODEL]
    h = (_rms(x_ref[...], g1_ref[...] * (1.0 + scale)) + shift).astype(BF16)

    v_raw = _dot(h, w_ref[:, OFF_B + D_MODEL:OFF_Q])
    u_raw = _dot(h, w_ref[:, OFF_B:OFF_B + D_MODEL])
    q_raw = _dot(h, w_ref[:, OFF_Q:OFF_K])

    v = _gelu_tanh(v_raw)
    u = _gelu_tanh(u_raw)
    mu = jnp.mean(v, axis=-1, keepdims=True)
    vc = v - mu
    var = jnp.mean(vc * vc, axis=-1, keepdims=True)
    vn = (vc * lax.rsqrt(var + EPS) * lng_ref[...] + lnb_ref[...]).astype(BF16)
    n_chunks = tm // SGU_CHUNK
    for g in range(SGU_GROUPS):
        cols = slice(g * LANES, (g + 1) * LANES)
        rhs = jnp.concatenate([vn[c * SGU_CHUNK:(c + 1) * SGU_CHUNK, cols] for c in range(n_chunks)], axis=1)
        mixed = _dot(ws_ref[g], rhs)
        for c in range(n_chunks):
            rows = slice(c * SGU_CHUNK, (c + 1) * SGU_CHUNK)
            yb_ref[rows, cols] = (u[rows, cols] * (mixed[:, c * LANES:(c + 1) * LANES] + bs_ref[g])).astype(BF16)

    k_raw = _dot(h, w_ref[:, OFF_K:OFF_V])
    vv = _dot(h, w_ref[:, OFF_V:OFF_G])

    lane = lax.broadcasted_iota(jnp.int32, (tm, LANES), 1)
    first_half = (lane % (2 * ROPE_FREQS)) < ROPE_FREQS
    low_head = lane < HEAD_DIM

    def head_norm_rope(z, g_ref, c_ref, s_ref, out_ref):
        zz = (z * z).astype(BF16)
        for blk in range(z.shape[1] // MXU_DIM):
            cols = slice(blk * MXU_DIM, (blk + 1) * MXU_DIM)
            ms = _dot(zz[:, cols], ones_ref[...])
            zn = z[:, cols] * lax.rsqrt(ms + EPS)
            for s in range(MXU_DIM // LANES):
                t = zn[:, s * LANES:(s + 1) * LANES] * g_ref[...]
                sw = jnp.where(first_half, pltpu.roll(t, LANES - ROPE_FREQS, 1),
                               pltpu.roll(t, ROPE_FREQS, 1))
                slab = blk * (MXU_DIM // LANES) + s
                out_ref(slab, t * c_ref[...] + sw * s_ref[...])

    def store_q(slab, val):
        q_ref[:, slab * LANES:(slab + 1) * LANES] = val.astype(BF16)

    def dup_heads(slab_val):
        r = pltpu.roll(slab_val, HEAD_DIM, 1)
        return jnp.where(low_head, slab_val, r), jnp.where(low_head, r, slab_val)

    def store_kd(slab, val):
        a, b = dup_heads(val)
        kd_ref[:, (2 * slab) * LANES:(2 * slab + 1) * LANES] = a.astype(BF16)
        kd_ref[:, (2 * slab + 1) * LANES:(2 * slab + 2) * LANES] = b.astype(BF16)

    head_norm_rope(q_raw, qg_ref, cq_ref, sq_ref, store_q)
    head_norm_rope(k_raw, kg_ref, ck_ref, sk_ref, store_kd)

    for s in range((OFF_G - OFF_V) // LANES):
        a, b = dup_heads(vv[:, s * LANES:(s + 1) * LANES])
        vd_ref[:, (2 * s) * LANES:(2 * s + 1) * LANES] = a.astype(BF16)
        vd_ref[:, (2 * s + 1) * LANES:(2 * s + 2) * LANES] = b.astype(BF16)

    xa_ref[...] = _dot(h, w_ref[:, 0:OFF_B])


def _inproj(layer, x2d, mod, g1, w_a, lng, lnb, ws, bs, qg, kg, ones_bd, tabs, *, seq, mod_base, per_batch, tm):
    n_tok = x2d.shape[0]
    tiles_per_seq = seq // tm
    tiles_per_row = tiles_per_seq if per_batch else n_tok // tm
    cq, sq, ck, sk = tabs
    const = lambda i: (0, 0)
    lay2 = lambda i: (layer, 0, 0)
    tab_map = lambda i: (i % tiles_per_seq, 0)
    kvw = 2 * N_KV_HEADS * HEAD_DIM
    return pl.pallas_call(
        functools.partial(_inproj_kernel, tm=tm, mod_base=mod_base, tiles_per_row=tiles_per_row),
        grid=(n_tok // tm,),
        in_specs=[
            pl.BlockSpec((tm, D_MODEL), lambda i: (i, 0)),
            pl.BlockSpec((None, MOD_ROWS, 6 * D_MODEL), lay2),
            pl.BlockSpec((1, D_MODEL), const),
            pl.BlockSpec((None, D_MODEL, OFF_G), lay2, pipeline_mode=pl.Buffered(1)),
            pl.BlockSpec((1, D_MODEL), const),
            pl.BlockSpec((1, D_MODEL), const),
            pl.BlockSpec((None, SGU_GROUPS, SGU_CHUNK, SGU_CHUNK), lambda i: (layer, 0, 0, 0)),
            pl.BlockSpec((SGU_GROUPS, SGU_CHUNK, LANES), lambda i: (0, 0, 0)),
            pl.BlockSpec((1, LANES), const),
            pl.BlockSpec((1, LANES), const),
            pl.BlockSpec((MXU_DIM, MXU_DIM), const),
            pl.BlockSpec((tm, LANES), tab_map),
            pl.BlockSpec((tm, LANES), tab_map),
            pl.BlockSpec((tm, LANES), tab_map),
            pl.BlockSpec((tm, LANES), tab_map),
        ],
        out_specs=[
            pl.BlockSpec((tm, D_MODEL), lambda i: (i, 0)),
            pl.BlockSpec((tm, D_MODEL), lambda i: (i, 0)),
            pl.BlockSpec((tm, D_MODEL), lambda i: (i, 0)),
            pl.BlockSpec((tm, kvw), lambda i: (i, 0)),
            pl.BlockSpec((tm, kvw), lambda i: (i, 0)),
        ],
        out_shape=[
            jax.ShapeDtypeStruct((n_tok, D_MODEL), F32),
            jax.ShapeDtypeStruct((n_tok, D_MODEL), BF16),
            jax.ShapeDtypeStruct((n_tok, D_MODEL), BF16),
            jax.ShapeDtypeStruct((n_tok, kvw), BF16),
            jax.ShapeDtypeStruct((n_tok, kvw), BF16),
        ],
        compiler_params=pltpu.CompilerParams(vmem_limit_bytes=_vmem_limit(56 << 20)),
        name="inproj",
    )(x2d, mod, g1, w_a, lng, lnb, ws, bs, qg, kg, ones_bd, cq, sq, ck, sk)


def _scan_kernel(*refs, tc, nchunk, nb, reverse, conv):
    if conv:
        (xa_ref, xp_ref, xn_ref, cw_ref, cb_ref, wa_ref, wx_ref, ba_ref, bx_ref, lam_ref, h0_ref,
         out_ref, hfin_ref, xc_ref, xe_ref, a_ref, u_ref, h_ref, hc_ref) = refs
    else:
        (xc_ref, hf_ref, wa_ref, wx_ref, ba_ref, bx_ref, lam_ref, h0_ref,
         out_ref, hfin_ref, a_ref, u_ref, h_ref, hc_ref) = refs
    i = pl.program_id(0)
    c = (nchunk - 1 - i) if reverse else i

    @pl.when(i == 0)
    def _():
        hc_ref[...] = h0_ref[...]

    z = -lam_ref[...]
    softplus = jnp.maximum(z, 0.0) + jnp.log1p(jnp.exp(-jnp.abs(z)))
    nc2l = (-0.5 * LRU_C * LOG2E) * softplus

    for b in range(nb):
        if conv:
            xe_ref[0:SUBLANES, :] = xp_ref[b] * (c > 0).astype(F32)
            xe_ref[SUBLANES:SUBLANES + tc, :] = xa_ref[b]
            xe_ref[SUBLANES + tc:2 * SUBLANES + tc, :] = xn_ref[b] * (c < nchunk - 1).astype(F32)
            xe = xe_ref[...]
            n_rows = tc + 2 * SUBLANES
            xc = cb_ref[...]
            for k in range(CONV_W):
                back = CONV_PAD_L - k
                xs = xe if back == 0 else pltpu.roll(xe, back % n_rows, 0)
                xc = xc + cw_ref[k:k + 1, :] * xs[SUBLANES:SUBLANES + tc]
            xc_ref[b] = xc
        else:
            xc = xc_ref[b]
        for blk in range(D_MODEL // MXU_DIM):
            cols = slice(blk * MXU_DIM, (blk + 1) * MXU_DIM)
            xcb = xc[:, cols]
            xcb16 = xcb.astype(BF16)
            t_r = jnp.tanh(_dot(xcb16, wa_ref[blk]) + ba_ref[:, cols])
            t_i = jnp.tanh(_dot(xcb16, wx_ref[blk]) + bx_ref[:, cols])
            log2_a = nc2l[:, cols] * t_r + nc2l[:, cols]
            a = jnp.exp2(log2_a)
            s = jnp.tanh(log2_a * (-LN2)) * (a * a + 1.0)
            root = jnp.where(s > 0.0, s * lax.rsqrt(s), 0.0)
            u = xcb * (t_i + 1.0) * root
            for sl in range(MXU_DIM // LANES):
                slab = blk * (MXU_DIM // LANES) + sl
                a_ref[b, pl.ds(slab, tc, stride=N_SLABS), :] = a[:, sl * LANES:(sl + 1) * LANES]
                u_ref[b, pl.ds(slab, tc, stride=N_SLABS), :] = u[:, sl * LANES:(sl + 1) * LANES]

    n_groups = tc // SCAN_GROUP

    def group(g, hs):
        gg = (n_groups - 1 - g) if reverse else g
        base = pl.multiple_of(gg * (SCAN_GROUP * N_SLABS), SCAN_GROUP * N_SLABS)
        hs = list(hs)
        for j in range(SCAN_GROUP):
            rows = pl.ds(base + ((SCAN_GROUP - 1 - j) if reverse else j) * N_SLABS, N_SLABS)
            for b in range(nb):
                hs[b] = a_ref[b, rows, :] * hs[b] + u_ref[b, rows, :]
                h_ref[b, rows, :] = hs[b]
        return tuple(hs)

    hs = lax.fori_loop(0, n_groups, group, tuple(hc_ref[b] for b in range(nb)))
    for b in range(nb):
        hc_ref[b] = hs[b]
        hfin_ref[b] = hs[b]
        for slab in range(N_SLABS):
            cols = slice(slab * LANES, (slab + 1) * LANES)
            h_tok = h_ref[b, pl.ds(slab, tc, stride=N_SLABS), :]
            if not conv:
                h_tok = h_tok + hf_ref[b, :, cols].astype(F32)
            out_ref[b, :, cols] = h_tok.astype(BF16)


def _scan(layer, direction, x_in, hf, cw, cb, wa_bd, wx_bd, ba, bx, lam, h0):
    nb, seq, _ = x_in.shape
    tc = 256
    nchunk = seq // tc
    conv = direction == 0
    reverse = direction == 1
    chunk = (lambda i: nchunk - 1 - i) if reverse else (lambda i: i)
    nblk = D_MODEL // MXU_DIM
    tok_spec = pl.BlockSpec((nb, tc, D_MODEL), lambda i: (0, chunk(i), 0))
    gate_w = pl.BlockSpec((None, None, nblk, MXU_DIM, MXU_DIM), lambda i: (layer, direction, 0, 0, 0))
    vec = pl.BlockSpec((None, None, 1, D_MODEL), lambda i: (layer, direction, 0, 0))
    state = pl.BlockSpec((None, nb, N_SLABS, LANES), lambda i: (direction, 0, 0, 0))
    in_specs = [tok_spec]
    args = [x_in]
    if conv:
        halo_blocks = seq // SUBLANES
        per = tc // SUBLANES
        in_specs += [
            pl.BlockSpec((nb, SUBLANES, D_MODEL), lambda i: (0, jnp.maximum(i * per - 1, 0), 0)),
            pl.BlockSpec((nb, SUBLANES, D_MODEL), lambda i: (0, jnp.minimum((i + 1) * per, halo_blocks - 1), 0)),
            pl.BlockSpec((None, CONV_W, D_MODEL), lambda i: (layer, 0, 0)),
            pl.BlockSpec((None, 1, D_MODEL), lambda i: (layer, 0, 0)),
        ]
        args += [x_in, x_in, cw, cb]
    else:
        in_specs.append(tok_spec)
        args.append(hf)
    in_specs += [gate_w, gate_w, vec, vec, vec, state]
    args += [wa_bd, wx_bd, ba, bx, lam, h0]
    out_specs = [
        pl.BlockSpec((nb, tc, D_MODEL), lambda i: (0, chunk(i), 0)),
        pl.BlockSpec((nb, N_SLABS, LANES), lambda i: (0, 0, 0)),
    ]
    out_shape = [
        jax.ShapeDtypeStruct((nb, seq, D_MODEL), BF16),
        jax.ShapeDtypeStruct((nb, N_SLABS, LANES), F32),
    ]
    scratch = []
    if conv:
        out_specs.append(pl.BlockSpec((nb, tc, D_MODEL), lambda i: (0, i, 0)))
        out_shape.append(jax.ShapeDtypeStruct((nb, seq, D_MODEL), F32))
        scratch.append(pltpu.VMEM((tc + 2 * SUBLANES, D_MODEL), F32))
    scratch += [pltpu.VMEM((nb, tc * N_SLABS, LANES), F32)] * 3 + [pltpu.VMEM((nb, N_SLABS, LANES), F32)]
    return pl.pallas_call(
        functools.partial(_scan_kernel, tc=tc, nchunk=nchunk, nb=nb, reverse=reverse, conv=conv),
        grid=(nchunk,),
        in_specs=in_specs,
        out_specs=out_specs,
        out_shape=out_shape,
        scratch_shapes=scratch,
        compiler_params=pltpu.CompilerParams(vmem_limit_bytes=_vmem_limit(56 << 20)),
        name="scan_fwd" if conv else "scan_bwd",
    )(*args)


def _attn_kernel(sink_ref, q_ref, *refs, n_steps, qb, band):
    if band:
        kp_ref, kcur_ref, kn_ref, vp_ref, vcur_ref, vn_ref, kx_ref, vx_ref, o_ref = refs
    else:
        kx_ref, vx_ref, o_ref = refs
    n = pl.program_id(1)
    lane = lax.broadcasted_iota(jnp.int32, (1, LANES), 1)
    low = lane < HEAD_DIM
    high = jnp.logical_not(low)

    def block_rows(j):
        return slice(j * ATT_BLOCK, (j + 1) * ATT_BLOCK)

    if band:
        qi = lax.broadcasted_iota(jnp.int32, (ATT_BLOCK, ATT_BLOCK), 0)
        kj = lax.broadcasted_iota(jnp.int32, (ATT_BLOCK, ATT_BLOCK), 1)

        def tiled(visible):
            one = jnp.where(visible, 0.0, NEG_BIG).astype(F32)
            return jnp.concatenate([one, one], axis=0)

        bias_prev = [tiled((kj >= qi) & (n > 0)) if j == 0 else tiled(kj >= qi) for j in range(qb)]
        bias_next = [tiled((kj <= qi) & (n < n_steps - 1)) if j == qb - 1 else tiled(kj <= qi) for j in range(qb)]

    def split(chain):
        j, rest = divmod(chain, 2 * N_KV_HEADS)
        kh, half = divmod(rest, 2)
        return j, kh, half

    def keys_values(j, kh, refs3, ctx_ref):
        slab = slice(kh * LANES, (kh + 1) * LANES)
        if not band:
            return ctx_ref[:, slab]
        before, cur, after = refs3
        blocks = [before[:, slab]] + [cur[block_rows(i), slab] for i in range(qb)] + [after[:, slab]]
        return jnp.concatenate(blocks[j:j + 3] + [ctx_ref[:, slab]], axis=0)

    def scores(chain):
        j, kh, half = split(chain)
        q2 = jnp.concatenate([q_ref[block_rows(j), (2 * kh) * LANES:(2 * kh + 1) * LANES],
                              q_ref[block_rows(j), (2 * kh + 1) * LANES:(2 * kh + 2) * LANES]], axis=0)
        kall = keys_values(j, kh, (kp_ref, kcur_ref, kn_ref) if band else None, kx_ref)
        khalf = jnp.where(low if half == 0 else high, kall, jnp.zeros_like(kall))
        return lax.dot_general(q2, khalf, (((1,), (1,)), ((), ())), preferred_element_type=F32)

    def weights(chain, logits):
        j, kh, half = split(chain)
        pieces = [logits[:, i * LANES:(i + 1) * LANES] for i in range(logits.shape[1] // LANES)]
        if band:
            pieces[0] = pieces[0] + bias_prev[j]
            pieces[2] = pieces[2] + bias_next[j]
        h_top = 4 * kh + half
        sk = jnp.concatenate([jnp.full((ATT_BLOCK, 1), sink_ref[h_top] * LOG2E, F32),
                              jnp.full((ATT_BLOCK, 1), sink_ref[h_top + 2] * LOG2E, F32)], axis=0)
        widest = pieces[0]
        for piece in pieces[1:]:
            widest = jnp.maximum(widest, piece)
        m = jnp.maximum(sk, jnp.max(widest, axis=-1, keepdims=True))
        p = jnp.concatenate([jnp.exp2(piece - m).astype(BF16) for piece in pieces], axis=1)
        return p, jnp.exp2(sk - m)

    def values(chain, p):
        j, kh, half = split(chain)
        vall = keys_values(j, kh, (vp_ref, vcur_ref, vn_ref) if band else None, vx_ref)
        vhalf = jnp.where(low if half == 0 else high, vall, jnp.ones_like(vall))
        return _dot(p, vhalf)

    def finish(j, kh, pv, sink_p):
        num = jnp.where(low, pv[0], pv[1])
        den = pltpu.roll(jnp.where(low, pv[1], pv[0]), HEAD_DIM, 1) + jnp.where(low, sink_p[0], sink_p[1])
        res = num / den
        for i in range(2):
            o_ref[block_rows(j), (2 * kh + i) * LANES:(2 * kh + i + 1) * LANES] = (
                res[i * ATT_BLOCK:(i + 1) * ATT_BLOCK].astype(BF16))

    n_chains = qb * 2 * N_KV_HEADS
    logits = {0: scores(0)}
    probs, sink_p, pv = {}, {}, {}
    for step in range(n_chains + 1):
        if step + 1 < n_chains:
            logits[step + 1] = scores(step + 1)
        if step < n_chains:
            probs[step], sink_p[step] = weights(step, logits.pop(step))
        c = step - 1
        if c >= 0:
            pv[c] = values(c, probs.pop(c))
            if c % 2 == 1:
                j, kh, _ = split(c)
                finish(j, kh, [pv.pop(c - 1), pv.pop(c)], [sink_p.pop(c - 1), sink_p.pop(c)])


def _attention(sink_l, q, kd, vd, kdx, vdx, *, band, qb=ATT_QB):
    nb, seq, _ = q.shape
    nctx = kdx.shape[1]
    nblk = seq // ATT_BLOCK
    qb = min(qb, nblk)
    assert nblk % qb == 0
    n_steps = nblk // qb
    kvw = kdx.shape[-1]
    qspec = pl.BlockSpec((None, qb * ATT_BLOCK, D_MODEL), lambda b, n: (b, n, 0))
    xspec = pl.BlockSpec((None, nctx, kvw), lambda b, n: (b, 0, 0))
    in_specs = [pl.BlockSpec(memory_space=pltpu.SMEM), qspec]
    args = [sink_l, q]
    if band:
        prev = pl.BlockSpec((None, ATT_BLOCK, kvw), lambda b, n: (b, jnp.maximum(n * qb - 1, 0), 0))
        cur = pl.BlockSpec((None, qb * ATT_BLOCK, kvw), lambda b, n: (b, n, 0))
        nxt = pl.BlockSpec((None, ATT_BLOCK, kvw), lambda b, n: (b, jnp.minimum((n + 1) * qb, nblk - 1), 0))
        in_specs += [prev, cur, nxt, prev, cur, nxt]
        args += [kd, kd, kd, vd, vd, vd]
    in_specs += [xspec, xspec]
    args += [kdx, vdx]
    return pl.pallas_call(
        functools.partial(_attn_kernel, n_steps=n_steps, qb=qb, band=band),
        grid=(nb, n_steps),
        in_specs=in_specs,
        out_specs=pl.BlockSpec((None, qb * ATT_BLOCK, D_MODEL), lambda b, n: (b, n, 0)),
        out_shape=jax.ShapeDtypeStruct((nb, seq, D_MODEL), BF16),
        compiler_params=pltpu.CompilerParams(vmem_limit_bytes=_vmem_limit(40 << 20)),
        name="attention_band" if band else "attention_ctx",
    )(*args)


def _merge_mlp_kernel(x_ref, mod_ref, g1_ref, g2_ref, wga_ref, wgb_ref, ya_ref, yb_ref, yc_ref, wb_ref, wo_ref,
                      w1_ref, w2_ref, o_ref, *, mod_base, tiles_per_row):
    i = pl.program_id(0)
    row = mod_base + i // tiles_per_row

    def mod(k):
        return mod_ref[pl.ds(row, 1), k * D_MODEL:(k + 1) * D_MODEL]

    def gate_logits(h1, k):
        parts = []
        for ref, base in ((wga_ref, 0), (wgb_ref, GATE_BLK)):
            lo = max(k * D_MODEL, base) - base
            hi = min((k + 1) * D_MODEL, base + GATE_BLK) - base
            if hi > lo:
                parts.append(_dot(h1, ref[:, lo:hi]))
        return parts[0] if len(parts) == 1 else jnp.concatenate(parts, axis=1)

    subs = [slice(j * MERGE_SUB, (j + 1) * MERGE_SUB) for j in range(x_ref.shape[0] // MERGE_SUB)]
    xs = [x_ref[r, :] for r in subs]
    ms = []
    for j, r in enumerate(subs):
        h1 = (_rms(xs[j], g1_ref[...] * (1.0 + mod(1))) + mod(0)).astype(BF16)
        branches = (ya_ref[r, :], yb_ref[r, :], yc_ref[r, :])
        m = None
        for k in range(3):
            gate = jax.nn.sigmoid(gate_logits(h1, k))
            term = gate * _dot(branches[k], wb_ref[k])
            m = term if m is None else m + term
        ms.append(m.astype(BF16))
    x1s = [xs[j] + mod(2) * _dot(ms[j], wo_ref[...]) for j in range(len(subs))]
    h2s = [(_rms(x1, g2_ref[...] * (1.0 + mod(4))) + mod(3)).astype(BF16) for x1 in x1s]
    acc = list(x1s)
    ff = D_FF // FF_CHUNKS
    for c in range(FF_CHUNKS):
        fs = [jnp.maximum(_dot(h2, w1_ref[:, c * ff:(c + 1) * ff]), 0.0) for h2 in h2s]
        for j in range(len(subs)):
            acc[j] = acc[j] + mod(5) * _dot((fs[j] * fs[j]).astype(BF16), w2_ref[c * ff:(c + 1) * ff, :])
    for j, r in enumerate(subs):
        o_ref[r, :] = acc[j]


def _merge_mlp(layer, x2d, mod, g1, g2, wg, ya, yb, yc, wb, wo, w1, w2, *, mod_base, tiles_per_row, tm):
    n_tok = x2d.shape[0]
    const = lambda i: (0, 0)
    lay2 = lambda i: (layer, 0, 0)
    once = pl.Buffered(1)
    tok = lambda i: (i, 0)
    return pl.pallas_call(
        functools.partial(_merge_mlp_kernel, mod_base=mod_base, tiles_per_row=tiles_per_row),
        grid=(n_tok // tm,),
        in_specs=[
            pl.BlockSpec((tm, D_MODEL), tok),
            pl.BlockSpec((None, MOD_ROWS, 6 * D_MODEL), lay2),
            pl.BlockSpec((1, D_MODEL), const),
            pl.BlockSpec((1, D_MODEL), const),
            pl.BlockSpec((None, D_MODEL, GATE_BLK), lambda i: (layer, 0, OFF_G // GATE_BLK), pipeline_mode=once),
            pl.BlockSpec((None, D_MODEL, GATE_BLK), lambda i: (layer, 0, OFF_G // GATE_BLK + 1), pipeline_mode=once),
            pl.BlockSpec((tm, D_MODEL), tok),
            pl.BlockSpec((tm, D_MODEL), tok),
            pl.BlockSpec((tm, D_MODEL), tok),
            pl.BlockSpec((None, 3, D_MODEL, D_MODEL), lambda i: (layer, 0, 0, 0), pipeline_mode=once),
            pl.BlockSpec((None, D_MODEL, D_MODEL), lay2, pipeline_mode=once),
            pl.BlockSpec((None, D_MODEL, D_FF), lay2, pipeline_mode=once),
            pl.BlockSpec((None, D_FF, D_MODEL), lay2, pipeline_mode=once),
        ],
        out_specs=pl.BlockSpec((tm, D_MODEL), tok),
        out_shape=jax.ShapeDtypeStruct((n_tok, D_MODEL), F32),
        compiler_params=pltpu.CompilerParams(vmem_limit_bytes=_vmem_limit(60 << 20)),
        name="merge_mlp",
    )(x2d, mod, g1, g2, wg, wg, ya, yb, yc, wb, wo, w1, w2)


def _rope_tables(seq):
    pos = jnp.arange(seq)
    row = (pos // GRID_W).astype(F32)
    col = (pos % GRID_W).astype(F32)
    inv = jnp.power(ROPE_BASE, -jnp.arange(ROPE_FREQS, dtype=F32) / ROPE_FREQS)
    ang_r = row[:, None] * inv
    ang_c = col[:, None] * inv
    cos = jnp.concatenate([jnp.cos(ang_r), jnp.cos(ang_r), jnp.cos(ang_c), jnp.cos(ang_c)], axis=-1)
    sin = jnp.concatenate([-jnp.sin(ang_r), jnp.sin(ang_r), -jnp.sin(ang_c), jnp.sin(ang_c)], axis=-1)
    reps = LANES // HEAD_DIM
    return jnp.tile(cos, (1, reps)), jnp.tile(sin, (1, reps))


def _block_diag(w):
    depth = w.shape[0]
    per = MXU_DIM // RNN_BLOCK
    w6 = w.reshape(depth, 2, RNN_BLOCKS // per, per, RNN_BLOCK, RNN_BLOCK)
    eye = jnp.eye(per, dtype=w.dtype)
    bd = jnp.einsum('ldcpij,pq->ldcpiqj', w6, eye)
    return bd.reshape(depth, 2, RNN_BLOCKS // per, MXU_DIM, MXU_DIM).astype(BF16)


def kernel(x, c, ctx, c_ctx, w_mod, b_mod, g_norm1, w_in, conv_w, conv_b, lru_wa, lru_ba, lru_wx, lru_bx,
           lru_lambda, sgu_ln_g, sgu_ln_b, sgu_w, sgu_b, q_norm_g, k_norm_g, sink, w_branch, w_out, g_norm2,
           w_ff1, w_ff2):
    n_batch, n_tok, _ = x.shape
    n_ctx = ctx.shape[1]
    depth = w_mod.shape[0]
    tm_x, tm_c = 512, 512
    assert n_batch + 1 <= MOD_ROWS and n_tok % tm_x == 0 and (n_batch * n_ctx) % tm_c == 0

    cond = jnp.zeros((MOD_ROWS, D_MODEL), F32).at[:n_batch].set(c).at[n_batch].set(c_ctx)
    mod = _modulation(cond, w_mod, b_mod)

    cos, sin = _rope_tables(n_tok)
    q_mul = ATT_SCALE * LOG2E
    tabs_x = (cos * q_mul, sin * q_mul, cos, sin)
    one = jnp.ones((n_batch * n_ctx, LANES), F32)
    zero = jnp.zeros((n_batch * n_ctx, LANES), F32)
    tabs_c = (one * q_mul, zero, one, zero)

    head = np.arange(MXU_DIM) // HEAD_DIM
    ones_bd = jnp.asarray((head[:, None] == head[None, :]) * (1.0 / HEAD_DIM), BF16)

    w_in16 = w_in.astype(BF16)
    wb16 = w_branch.astype(BF16)
    wo16 = w_out.astype(BF16)
    w116 = w_ff1.astype(BF16)
    w216 = w_ff2.astype(BF16)
    ws16 = sgu_w.astype(BF16)
    wa_bd = _block_diag(lru_wa)
    wx_bd = _block_diag(lru_wx)
    ba_h = (0.5 * lru_ba)[:, :, None]
    bx_h = (0.5 * lru_bx)[:, :, None]
    lam = lru_lambda[:, :, None]
    conv_wh = 0.5 * conv_w
    conv_bh = (0.5 * conv_b)[:, None]

    x2d = x.reshape(n_batch * n_tok, D_MODEL)
    cx2d = ctx.reshape(n_batch * n_ctx, D_MODEL)
    h0 = jnp.zeros((2, n_batch, N_SLABS, LANES), F32)
    reps = LANES // HEAD_DIM

    def rglru(layer, xa, init):
        scan_p = (wa_bd, wx_bd, ba_h, bx_h, lam)
        hf, hf_fin, xc = _scan(layer, 0, xa, None, conv_wh, conv_bh, *scan_p, init)
        ya, hb_fin = _scan(layer, 1, xc, hf, None, None, *scan_p, init)
        return ya, jnp.stack([hf_fin, hb_fin])

    for l in range(depth):
        last = l == depth - 1
        g1 = g_norm1[l][None]
        g2 = g_norm2[l][None]
        bs = jnp.broadcast_to(sgu_b[l][:, :, None], (SGU_GROUPS, SGU_CHUNK, LANES))
        qg = jnp.tile(q_norm_g[l], reps)[None]
        kg = jnp.tile(k_norm_g[l], reps)[None]
        inproj_p = (mod, g1, w_in16, sgu_ln_g[l][None], sgu_ln_b[l][None], ws16, bs, qg, kg, ones_bd)
        merge_w = (wb16, wo16, w116, w216)

        xa_c, yb_c, q_c, kd_c, vd_c = _inproj(l, cx2d, *inproj_p, tabs_c, seq=n_batch * n_ctx, mod_base=n_batch,
                                              per_batch=False, tm=tm_c)
        ya_c, hfin_c = rglru(l, xa_c.reshape(n_batch, n_ctx, D_MODEL), h0)
        kd_c = kd_c.reshape(n_batch, n_ctx, -1)
        vd_c = vd_c.reshape(n_batch, n_ctx, -1)

        xa, yb, q, kd, vd = _inproj(l, x2d, *inproj_p, tabs_x, seq=n_tok, mod_base=0, per_batch=True, tm=tm_x)
        ya_x, _ = rglru(l, xa.reshape(n_batch, n_tok, D_MODEL), hfin_c)
        yc = _attention(sink[l], q.reshape(n_batch, n_tok, D_MODEL), kd.reshape(n_batch, n_tok, -1),
                        vd.reshape(n_batch, n_tok, -1), kd_c, vd_c, band=True)
        x2d = _merge_mlp(l, x2d, mod, g1, g2, w_in16, ya_x.reshape(n_batch * n_tok, D_MODEL), yb,
                         yc.reshape(n_batch * n_tok, D_MODEL), *merge_w,
                         mod_base=0, tiles_per_row=n_tok // tm_x, tm=tm_x)

        if not last:
            yc_c = _attention(sink[l], q_c.reshape(n_batch, n_ctx, D_MODEL), None, None, kd_c, vd_c, band=False)
            cx2d = _merge_mlp(l, cx2d, mod, g1, g2, w_in16, ya_c.reshape(n_batch * n_ctx, D_MODEL), yb_c,
                              yc_c.reshape(n_batch * n_ctx, D_MODEL), *merge_w,
                              mod_base=n_batch, tiles_per_row=n_batch * n_ctx // tm_c, tm=tm_c)

    return x2d.reshape(n_batch, n_tok, D_MODEL)
```

```python
import functools

import numpy as np
import jax
import jax.numpy as jnp
from jax import lax
from jax.experimental import pallas as pl
from jax.experimental.pallas import tpu as pltpu

F32 = jnp.float32
BF16 = jnp.bfloat16

D_MODEL = 1024
GRID_W = 64
EPS = 1e-6
RNN_BLOCKS = 16
RNN_BLOCK = D_MODEL // RNN_BLOCKS
CONV_W = 4
CONV_PAD_L = 2
LRU_C = 8.0
SGU_CHUNK = 128
SGU_GROUPS = 8
N_HEADS = 16
N_KV_HEADS = 4
HEAD_DIM = 64
WINDOW = 128
ATT_BLOCK = 128
ATT_SCALE = HEAD_DIM ** -0.5
ROPE_BASE = 10000.0
ROPE_FREQS = HEAD_DIM // 4
D_FF = 4 * D_MODEL
OFF_B = D_MODEL
OFF_Q = OFF_B + 2 * D_MODEL
OFF_K = OFF_Q + N_HEADS * HEAD_DIM
OFF_V = OFF_K + N_KV_HEADS * HEAD_DIM
OFF_G = OFF_V + N_KV_HEADS * HEAD_DIM

LANES = 128
SUBLANES = 8
MXU_DIM = 256
VMEM_BYTES = 64 * 1024 * 1024

N_SLABS = D_MODEL // LANES
MOD_ROWS = SUBLANES
NEG_BIG = -1e30
LOG2E = 1.4426950408889634
LN2 = 0.6931471805599453
SCAN_GROUP = 16
GATE_BLK = 1536
MERGE_SUB = 256
FF_CHUNKS = 2
ATT_QB = 8


def _vmem_limit(nbytes):
    return int(min(nbytes, VMEM_BYTES - 4 * 1024 * 1024))


def _rms(x, g):
    ms = jnp.mean(x * x, axis=-1, keepdims=True)
    return x * lax.rsqrt(ms + EPS) * g


def _gelu_tanh(x):
    c = np.sqrt(2.0 / np.pi).astype(np.float32)
    inner = x * (c + (0.044715 * c) * (x * x))
    hx = 0.5 * x
    return hx + hx * jnp.tanh(inner)


def _dot(a, b):
    return jnp.dot(a, b, preferred_element_type=F32)


def _mod_kernel(c_ref, w_ref, b_ref, o_ref):
    c = c_ref[...]
    s = c * jax.nn.sigmoid(c)
    o_ref[0] = _dot(s.astype(BF16), w_ref[0].astype(BF16)) + b_ref[0]


def _modulation(cond, w_mod, b_mod):
    depth, _, width = w_mod.shape
    tn = 1536
    return pl.pallas_call(
        _mod_kernel,
        grid=(depth, width // tn),
        in_specs=[
            pl.BlockSpec((MOD_ROWS, D_MODEL), lambda l, j: (0, 0)),
            pl.BlockSpec((1, D_MODEL, tn), lambda l, j: (l, 0, j)),
            pl.BlockSpec((1, 1, tn), lambda l, j: (l, 0, j)),
        ],
        out_specs=pl.BlockSpec((1, MOD_ROWS, tn), lambda l, j: (l, 0, j)),
        out_shape=jax.ShapeDtypeStruct((depth, MOD_ROWS, width), F32),
        compiler_params=pltpu.CompilerParams(vmem_limit_bytes=_vmem_limit(40 << 20)),
        name="modulation",
    )(cond, w_mod, b_mod.reshape(depth, 1, width))


def _inproj_kernel(x_ref, mod_ref, g1_ref, w_ref, lng_ref, lnb_ref, ws_ref, bs_ref, qg_ref, kg_ref,
                   ones_ref, cq_ref, sq_ref, ck_ref, sk_ref,
                   xa_ref, yb_ref, q_ref, kd_ref, vd_ref, *, tm, mod_base, tiles_per_row):
    i = pl.program_id(0)
    row = mod_base + i // tiles_per_row
    shift = mod_ref[pl.ds(row, 1), 0:D_MODEL]
    scale = mod_ref[pl.ds(row, 1), D_MODEL:2 * D_MODEL]
    h = (_rms(x_ref[...], g1_ref[...] * (1.0 + scale)) + shift).astype(BF16)

    v_raw = _dot(h, w_ref[:, OFF_B + D_MODEL:OFF_Q])
    u_raw = _dot(h, w_ref[:, OFF_B:OFF_B + D_MODEL])
    q_raw = _dot(h, w_ref[:, OFF_Q:OFF_K])

    v = _gelu_tanh(v_raw)
    u = _gelu_tanh(u_raw)
    mu = jnp.mean(v, axis=-1, keepdims=True)
    vc = v - mu
    var = jnp.mean(vc * vc, axis=-1, keepdims=True)
    vn = (vc * lax.rsqrt(var + EPS) * lng_ref[...] + lnb_ref[...]).astype(BF16)
    n_chunks = tm // SGU_CHUNK
    for g in range(SGU_GROUPS):
        cols = slice(g * LANES, (g + 1) * LANES)
        rhs = jnp.concatenate([vn[c * SGU_CHUNK:(c + 1) * SGU_CHUNK, cols] for c in range(n_chunks)], axis=1)
        mixed = _dot(ws_ref[g], rhs)
        for c in range(n_chunks):
            rows = slice(c * SGU_CHUNK, (c + 1) * SGU_CHUNK)
            yb_ref[rows, cols] = (u[rows, cols] * (mixed[:, c * LANES:(c + 1) * LANES] + bs_ref[g])).astype(BF16)

    k_raw = _dot(h, w_ref[:, OFF_K:OFF_V])
    vv = _dot(h, w_ref[:, OFF_V:OFF_G])

    lane = lax.broadcasted_iota(jnp.int32, (tm, LANES), 1)
    first_half = (lane % (2 * ROPE_FREQS)) < ROPE_FREQS
    low_head = lane < HEAD_DIM

    def head_norm_rope(z, g_ref, c_ref, s_ref, out_ref):
        zz = (z * z).astype(BF16)
        for blk in range(z.shape[1] // MXU_DIM):
            cols = slice(blk * MXU_DIM, (blk + 1) * MXU_DIM)
            ms = _dot(zz[:, cols], ones_ref[...])
            zn = z[:, cols] * lax.rsqrt(ms + EPS)
            for s in range(MXU_DIM // LANES):
                t = zn[:, s * LANES:(s + 1) * LANES] * g_ref[...]
                sw = jnp.where(first_half, pltpu.roll(t, LANES - ROPE_FREQS, 1),
                               pltpu.roll(t, ROPE_FREQS, 1))
                slab = blk * (MXU_DIM // LANES) + s
                out_ref(slab, t * c_ref[...] + sw * s_ref[...])

    def store_q(slab, val):
        q_ref[:, slab * LANES:(slab + 1) * LANES] = val.astype(BF16)

    def dup_heads(slab_val):
        r = pltpu.roll(slab_val, HEAD_DIM, 1)
        return jnp.where(low_head, slab_val, r), jnp.where(low_head, r, slab_val)

    def store_kd(slab, val):
        a, b = dup_heads(val)
        kd_ref[:, (2 * slab) * LANES:(2 * slab + 1) * LANES] = a.astype(BF16)
        kd_ref[:, (2 * slab + 1) * LANES:(2 * slab + 2) * LANES] = b.astype(BF16)

    head_norm_rope(q_raw, qg_ref, cq_ref, sq_ref, store_q)
    head_norm_rope(k_raw, kg_ref, ck_ref, sk_ref, store_kd)

    for s in range((OFF_G - OFF_V) // LANES):
        a, b = dup_heads(vv[:, s * LANES:(s + 1) * LANES])
        vd_ref[:, (2 * s) * LANES:(2 * s + 1) * LANES] = a.astype(BF16)
        vd_ref[:, (2 * s + 1) * LANES:(2 * s + 2) * LANES] = b.astype(BF16)

    xa_ref[...] = _dot(h, w_ref[:, 0:OFF_B])


def _inproj(layer, x2d, mod, g1, w_a, lng, lnb, ws, bs, qg, kg, ones_bd, tabs, *, seq, mod_base, per_batch, tm):
    n_tok = x2d.shape[0]
    tiles_per_seq = seq // tm
    tiles_per_row = tiles_per_seq if per_batch else n_tok // tm
    cq, sq, ck, sk = tabs
    const = lambda i: (0, 0)
    lay2 = lambda i: (layer, 0, 0)
    tab_map = lambda i: (i % tiles_per_seq, 0)
    kvw = 2 * N_KV_HEADS * HEAD_DIM
    return pl.pallas_call(
        functools.partial(_inproj_kernel, tm=tm, mod_base=mod_base, tiles_per_row=tiles_per_row),
        grid=(n_tok // tm,),
        in_specs=[
            pl.BlockSpec((tm, D_MODEL), lambda i: (i, 0)),
            pl.BlockSpec((None, MOD_ROWS, 6 * D_MODEL), lay2),
            pl.BlockSpec((None, 1, D_MODEL), lay2),
            pl.BlockSpec((None, D_MODEL, OFF_G), lay2, pipeline_mode=pl.Buffered(1)),
            pl.BlockSpec((None, 1, D_MODEL), lay2),
            pl.BlockSpec((None, 1, D_MODEL), lay2),
            pl.BlockSpec((None, SGU_GROUPS, SGU_CHUNK, SGU_CHUNK), lambda i: (layer, 0, 0, 0)),
            pl.BlockSpec((None, SGU_GROUPS, SGU_CHUNK, LANES), lambda i: (layer, 0, 0, 0)),
            pl.BlockSpec((None, 1, LANES), lay2),
            pl.BlockSpec((None, 1, LANES), lay2),
            pl.BlockSpec((MXU_DIM, MXU_DIM), const),
            pl.BlockSpec((tm, LANES), tab_map),
            pl.BlockSpec((tm, LANES), tab_map),
            pl.BlockSpec((tm, LANES), tab_map),
            pl.BlockSpec((tm, LANES), tab_map),
        ],
        out_specs=[
            pl.BlockSpec((tm, D_MODEL), lambda i: (i, 0)),
            pl.BlockSpec((tm, D_MODEL), lambda i: (i, 0)),
            pl.BlockSpec((tm, D_MODEL), lambda i: (i, 0)),
            pl.BlockSpec((tm, kvw), lambda i: (i, 0)),
            pl.BlockSpec((tm, kvw), lambda i: (i, 0)),
        ],
        out_shape=[
            jax.ShapeDtypeStruct((n_tok, D_MODEL), F32),
            jax.ShapeDtypeStruct((n_tok, D_MODEL), BF16),
            jax.ShapeDtypeStruct((n_tok, D_MODEL), BF16),
            jax.ShapeDtypeStruct((n_tok, kvw), BF16),
            jax.ShapeDtypeStruct((n_tok, kvw), BF16),
        ],
        compiler_params=pltpu.CompilerParams(vmem_limit_bytes=_vmem_limit(56 << 20)),
        name="inproj",
    )(x2d, mod, g1, w_a, lng, lnb, ws, bs, qg, kg, ones_bd, cq, sq, ck, sk)


def _scan_kernel(*refs, tc, nchunk, nb, reverse, conv):
    if conv:
        (xa_ref, xp_ref, xn_ref, cw_ref, cb_ref, wa_ref, wx_ref, ba_ref, bx_ref, lam_ref, h0_ref,
         out_ref, hfin_ref, xc_ref, xe_ref, a_ref, u_ref, h_ref, hc_ref) = refs
    else:
        (xc_ref, hf_ref, wa_ref, wx_ref, ba_ref, bx_ref, lam_ref, h0_ref,
         out_ref, hfin_ref, a_ref, u_ref, h_ref, hc_ref) = refs
    i = pl.program_id(0)
    c = (nchunk - 1 - i) if reverse else i

    @pl.when(i == 0)
    def _():
        hc_ref[...] = h0_ref[...]

    z = -lam_ref[...]
    softplus = jnp.maximum(z, 0.0) + jnp.log1p(jnp.exp(-jnp.abs(z)))
    nc2l = (-0.5 * LRU_C * LOG2E) * softplus

    for b in range(nb):
        if conv:
            xe_ref[0:SUBLANES, :] = xp_ref[b] * (c > 0).astype(F32)
            xe_ref[SUBLANES:SUBLANES + tc, :] = xa_ref[b]
            xe_ref[SUBLANES + tc:2 * SUBLANES + tc, :] = xn_ref[b] * (c < nchunk - 1).astype(F32)
            xe = xe_ref[...]
            n_rows = tc + 2 * SUBLANES
            xc = cb_ref[...]
            for k in range(CONV_W):
                back = CONV_PAD_L - k
                xs = xe if back == 0 else pltpu.roll(xe, back % n_rows, 0)
                xc = xc + cw_ref[k:k + 1, :] * xs[SUBLANES:SUBLANES + tc]
            xc_ref[b] = xc
        else:
            xc = xc_ref[b]
        for blk in range(D_MODEL // MXU_DIM):
            cols = slice(blk * MXU_DIM, (blk + 1) * MXU_DIM)
            xcb = xc[:, cols]
            xcb16 = xcb.astype(BF16)
            t_r = jnp.tanh(_dot(xcb16, wa_ref[blk]) + ba_ref[:, cols])
            t_i = jnp.tanh(_dot(xcb16, wx_ref[blk]) + bx_ref[:, cols])
            log2_a = nc2l[:, cols] * t_r + nc2l[:, cols]
            a = jnp.exp2(log2_a)
            s = jnp.tanh(log2_a * (-LN2)) * (a * a + 1.0)
            root = jnp.where(s > 0.0, s * lax.rsqrt(s), 0.0)
            u = xcb * (t_i + 1.0) * root
            for sl in range(MXU_DIM // LANES):
                slab = blk * (MXU_DIM // LANES) + sl
                a_ref[b, pl.ds(slab, tc, stride=N_SLABS), :] = a[:, sl * LANES:(sl + 1) * LANES]
                u_ref[b, pl.ds(slab, tc, stride=N_SLABS), :] = u[:, sl * LANES:(sl + 1) * LANES]

    n_groups = tc // SCAN_GROUP

    def group(g, hs):
        gg = (n_groups - 1 - g) if reverse else g
        base = pl.multiple_of(gg * (SCAN_GROUP * N_SLABS), SCAN_GROUP * N_SLABS)
        hs = list(hs)
        for j in range(SCAN_GROUP):
            rows = pl.ds(base + ((SCAN_GROUP - 1 - j) if reverse else j) * N_SLABS, N_SLABS)
            for b in range(nb):
                hs[b] = a_ref[b, rows, :] * hs[b] + u_ref[b, rows, :]
                h_ref[b, rows, :] = hs[b]
        return tuple(hs)

    hs = lax.fori_loop(0, n_groups, group, tuple(hc_ref[b] for b in range(nb)))
    for b in range(nb):
        hc_ref[b] = hs[b]
        hfin_ref[b] = hs[b]
        for slab in range(N_SLABS):
            cols = slice(slab * LANES, (slab + 1) * LANES)
            h_tok = h_ref[b, pl.ds(slab, tc, stride=N_SLABS), :]
            if not conv:
                h_tok = h_tok + hf_ref[b, :, cols].astype(F32)
            out_ref[b, :, cols] = h_tok.astype(BF16)


def _scan(layer, direction, x_in, hf, cw, cb, wa_bd, wx_bd, ba, bx, lam, h0):
    nb, seq, _ = x_in.shape
    tc = 256
    nchunk = seq // tc
    conv = direction == 0
    reverse = direction == 1
    chunk = (lambda i: nchunk - 1 - i) if reverse else (lambda i: i)
    nblk = D_MODEL // MXU_DIM
    tok_spec = pl.BlockSpec((nb, tc, D_MODEL), lambda i: (0, chunk(i), 0))
    gate_w = pl.BlockSpec((None, None, nblk, MXU_DIM, MXU_DIM), lambda i: (layer, direction, 0, 0, 0))
    vec = pl.BlockSpec((None, None, 1, D_MODEL), lambda i: (layer, direction, 0, 0))
    state = pl.BlockSpec((None, nb, N_SLABS, LANES), lambda i: (direction, 0, 0, 0))
    in_specs = [tok_spec]
    args = [x_in]
    if conv:
        halo_blocks = seq // SUBLANES
        per = tc // SUBLANES
        in_specs += [
            pl.BlockSpec((nb, SUBLANES, D_MODEL), lambda i: (0, jnp.maximum(i * per - 1, 0), 0)),
            pl.BlockSpec((nb, SUBLANES, D_MODEL), lambda i: (0, jnp.minimum((i + 1) * per, halo_blocks - 1), 0)),
            pl.BlockSpec((None, CONV_W, D_MODEL), lambda i: (layer, 0, 0)),
            pl.BlockSpec((None, 1, D_MODEL), lambda i: (layer, 0, 0)),
        ]
        args += [x_in, x_in, cw, cb]
    else:
        in_specs.append(tok_spec)
        args.append(hf)
    in_specs += [gate_w, gate_w, vec, vec, vec, state]
    args += [wa_bd, wx_bd, ba, bx, lam, h0]
    out_specs = [
        pl.BlockSpec((nb, tc, D_MODEL), lambda i: (0, chunk(i), 0)),
        pl.BlockSpec((nb, N_SLABS, LANES), lambda i: (0, 0, 0)),
    ]
    out_shape = [
        jax.ShapeDtypeStruct((nb, seq, D_MODEL), BF16),
        jax.ShapeDtypeStruct((nb, N_SLABS, LANES), F32),
    ]
    scratch = []
    if conv:
        out_specs.append(pl.BlockSpec((nb, tc, D_MODEL), lambda i: (0, i, 0)))
        out_shape.append(jax.ShapeDtypeStruct((nb, seq, D_MODEL), F32))
        scratch.append(pltpu.VMEM((tc + 2 * SUBLANES, D_MODEL), F32))
    scratch += [pltpu.VMEM((nb, tc * N_SLABS, LANES), F32)] * 3 + [pltpu.VMEM((nb, N_SLABS, LANES), F32)]
    return pl.pallas_call(
        functools.partial(_scan_kernel, tc=tc, nchunk=nchunk, nb=nb, reverse=reverse, conv=conv),
        grid=(nchunk,),
        in_specs=in_specs,
        out_specs=out_specs,
        out_shape=out_shape,
        scratch_shapes=scratch,
        compiler_params=pltpu.CompilerParams(vmem_limit_bytes=_vmem_limit(56 << 20)),
        name="scan_fwd" if conv else "scan_bwd",
    )(*args)


def _attn_kernel(sink_ref, q_ref, *refs, layer, n_steps, qb, band):
    if band:
        kp_ref, kcur_ref, kn_ref, vp_ref, vcur_ref, vn_ref, kx_ref, vx_ref, o_ref = refs
    else:
        kx_ref, vx_ref, o_ref = refs
    n = pl.program_id(1)
    lane = lax.broadcasted_iota(jnp.int32, (1, LANES), 1)
    low = lane < HEAD_DIM
    high = jnp.logical_not(low)

    def block_rows(j):
        return slice(j * ATT_BLOCK, (j + 1) * ATT_BLOCK)

    if band:
        qi = lax.broadcasted_iota(jnp.int32, (ATT_BLOCK, ATT_BLOCK), 0)
        kj = lax.broadcasted_iota(jnp.int32, (ATT_BLOCK, ATT_BLOCK), 1)

        def tiled(visible):
            one = jnp.where(visible, 0.0, NEG_BIG).astype(F32)
            return jnp.concatenate([one, one], axis=0)

        bias_prev = [tiled((kj >= qi) & (n > 0)) if j == 0 else tiled(kj >= qi) for j in range(qb)]
        bias_next = [tiled((kj <= qi) & (n < n_steps - 1)) if j == qb - 1 else tiled(kj <= qi) for j in range(qb)]

    def split(chain):
        j, rest = divmod(chain, 2 * N_KV_HEADS)
        kh, half = divmod(rest, 2)
        return j, kh, half

    def keys_values(j, kh, refs3, ctx_ref):
        slab = slice(kh * LANES, (kh + 1) * LANES)
        if not band:
            return ctx_ref[:, slab]
        before, cur, after = refs3
        blocks = [before[:, slab]] + [cur[block_rows(i), slab] for i in range(qb)] + [after[:, slab]]
        return jnp.concatenate(blocks[j:j + 3] + [ctx_ref[:, slab]], axis=0)

    def scores(chain):
        j, kh, half = split(chain)
        q2 = jnp.concatenate([q_ref[block_rows(j), (2 * kh) * LANES:(2 * kh + 1) * LANES],
                              q_ref[block_rows(j), (2 * kh + 1) * LANES:(2 * kh + 2) * LANES]], axis=0)
        kall = keys_values(j, kh, (kp_ref, kcur_ref, kn_ref) if band else None, kx_ref)
        khalf = jnp.where(low if half == 0 else high, kall, jnp.zeros_like(kall))
        return lax.dot_general(q2, khalf, (((1,), (1,)), ((), ())), preferred_element_type=F32)

    def weights(chain, logits):
        j, kh, half = split(chain)
        pieces = [logits[:, i * LANES:(i + 1) * LANES] for i in range(logits.shape[1] // LANES)]
        if band:
            pieces[0] = pieces[0] + bias_prev[j]
            pieces[2] = pieces[2] + bias_next[j]
        h_top = 4 * kh + half
        sk = jnp.concatenate([jnp.full((ATT_BLOCK, 1), sink_ref[layer, h_top] * LOG2E, F32),
                              jnp.full((ATT_BLOCK, 1), sink_ref[layer, h_top + 2] * LOG2E, F32)], axis=0)
        widest = pieces[0]
        for piece in pieces[1:]:
            widest = jnp.maximum(widest, piece)
        m = jnp.maximum(sk, jnp.max(widest, axis=-1, keepdims=True))
        p = jnp.concatenate([jnp.exp2(piece - m).astype(BF16) for piece in pieces], axis=1)
        return p, jnp.exp2(sk - m)

    def values(chain, p):
        j, kh, half = split(chain)
        vall = keys_values(j, kh, (vp_ref, vcur_ref, vn_ref) if band else None, vx_ref)
        vhalf = jnp.where(low if half == 0 else high, vall, jnp.ones_like(vall))
        return _dot(p, vhalf)

    def finish(j, kh, pv, sink_p):
        num = jnp.where(low, pv[0], pv[1])
        den = pltpu.roll(jnp.where(low, pv[1], pv[0]), HEAD_DIM, 1) + jnp.where(low, sink_p[0], sink_p[1])
        res = num / den
        for i in range(2):
            o_ref[block_rows(j), (2 * kh + i) * LANES:(2 * kh + i + 1) * LANES] = (
                res[i * ATT_BLOCK:(i + 1) * ATT_BLOCK].astype(BF16))

    n_chains = qb * 2 * N_KV_HEADS
    logits = {0: scores(0)}
    probs, sink_p, pv = {}, {}, {}
    for step in range(n_chains + 1):
        if step + 1 < n_chains:
            logits[step + 1] = scores(step + 1)
        if step < n_chains:
            probs[step], sink_p[step] = weights(step, logits.pop(step))
        c = step - 1
        if c >= 0:
            pv[c] = values(c, probs.pop(c))
            if c % 2 == 1:
                j, kh, _ = split(c)
                finish(j, kh, [pv.pop(c - 1), pv.pop(c)], [sink_p.pop(c - 1), sink_p.pop(c)])


def _attention(layer, sink, q, kd, vd, kdx, vdx, *, band, qb=ATT_QB):
    nb, seq, _ = q.shape
    nctx = kdx.shape[1]
    nblk = seq // ATT_BLOCK
    qb = min(qb, nblk)
    assert nblk % qb == 0
    n_steps = nblk // qb
    kvw = kdx.shape[-1]
    qspec = pl.BlockSpec((None, qb * ATT_BLOCK, D_MODEL), lambda b, n: (b, n, 0))
    xspec = pl.BlockSpec((None, nctx, kvw), lambda b, n: (b, 0, 0))
    in_specs = [pl.BlockSpec(memory_space=pltpu.SMEM), qspec]
    args = [sink, q]
    if band:
        prev = pl.BlockSpec((None, ATT_BLOCK, kvw), lambda b, n: (b, jnp.maximum(n * qb - 1, 0), 0))
        cur = pl.BlockSpec((None, qb * ATT_BLOCK, kvw), lambda b, n: (b, n, 0))
        nxt = pl.BlockSpec((None, ATT_BLOCK, kvw), lambda b, n: (b, jnp.minimum((n + 1) * qb, nblk - 1), 0))
        in_specs += [prev, cur, nxt, prev, cur, nxt]
        args += [kd, kd, kd, vd, vd, vd]
    in_specs += [xspec, xspec]
    args += [kdx, vdx]
    return pl.pallas_call(
        functools.partial(_attn_kernel, layer=layer, n_steps=n_steps, qb=qb, band=band),
        grid=(nb, n_steps),
        in_specs=in_specs,
        out_specs=pl.BlockSpec((None, qb * ATT_BLOCK, D_MODEL), lambda b, n: (b, n, 0)),
        out_shape=jax.ShapeDtypeStruct((nb, seq, D_MODEL), BF16),
        compiler_params=pltpu.CompilerParams(vmem_limit_bytes=_vmem_limit(40 << 20)),
        name="attention_band" if band else "attention_ctx",
    )(*args)


def _merge_mlp_kernel(x_ref, mod_ref, g1_ref, g2_ref, wga_ref, wgb_ref, ya_ref, yb_ref, yc_ref, wb_ref, wo_ref,
                      w1_ref, w2_ref, o_ref, *, mod_base, tiles_per_row):
    i = pl.program_id(0)
    row = mod_base + i // tiles_per_row

    def mod(k):
        return mod_ref[pl.ds(row, 1), k * D_MODEL:(k + 1) * D_MODEL]

    def gate_logits(h1, k):
        parts = []
        for ref, base in ((wga_ref, 0), (wgb_ref, GATE_BLK)):
            lo = max(k * D_MODEL, base) - base
            hi = min((k + 1) * D_MODEL, base + GATE_BLK) - base
            if hi > lo:
                parts.append(_dot(h1, ref[:, lo:hi]))
        return parts[0] if len(parts) == 1 else jnp.concatenate(parts, axis=1)

    subs = [slice(j * MERGE_SUB, (j + 1) * MERGE_SUB) for j in range(x_ref.shape[0] // MERGE_SUB)]
    xs = [x_ref[r, :] for r in subs]
    ms = []
    for j, r in enumerate(subs):
        h1 = (_rms(xs[j], g1_ref[...] * (1.0 + mod(1))) + mod(0)).astype(BF16)
        branches = (ya_ref[r, :], yb_ref[r, :], yc_ref[r, :])
        m = None
        for k in range(3):
            gate = jax.nn.sigmoid(gate_logits(h1, k))
            term = gate * _dot(branches[k], wb_ref[k])
            m = term if m is None else m + term
        ms.append(m.astype(BF16))
    x1s = [xs[j] + mod(2) * _dot(ms[j], wo_ref[...]) for j in range(len(subs))]
    h2s = [(_rms(x1, g2_ref[...] * (1.0 + mod(4))) + mod(3)).astype(BF16) for x1 in x1s]
    acc = list(x1s)
    ff = D_FF // FF_CHUNKS
    for c in range(FF_CHUNKS):
        fs = [jnp.maximum(_dot(h2, w1_ref[:, c * ff:(c + 1) * ff]), 0.0) for h2 in h2s]
        for j in range(len(subs)):
            acc[j] = acc[j] + mod(5) * _dot((fs[j] * fs[j]).astype(BF16), w2_ref[c * ff:(c + 1) * ff, :])
    for j, r in enumerate(subs):
        o_ref[r, :] = acc[j]


def _merge_mlp(layer, x2d, mod, g1, g2, wg, ya, yb, yc, wb, wo, w1, w2, *, mod_base, tiles_per_row, tm):
    n_tok = x2d.shape[0]
    const = lambda i: (0, 0)
    lay2 = lambda i: (layer, 0, 0)
    once = pl.Buffered(1)
    tok = lambda i: (i, 0)
    return pl.pallas_call(
        functools.partial(_merge_mlp_kernel, mod_base=mod_base, tiles_per_row=tiles_per_row),
        grid=(n_tok // tm,),
        in_specs=[
            pl.BlockSpec((tm, D_MODEL), tok),
            pl.BlockSpec((None, MOD_ROWS, 6 * D_MODEL), lay2),
            pl.BlockSpec((None, 1, D_MODEL), lay2),
            pl.BlockSpec((None, 1, D_MODEL), lay2),
            pl.BlockSpec((None, D_MODEL, GATE_BLK), lambda i: (layer, 0, OFF_G // GATE_BLK), pipeline_mode=once),
            pl.BlockSpec((None, D_MODEL, GATE_BLK), lambda i: (layer, 0, OFF_G // GATE_BLK + 1), pipeline_mode=once),
            pl.BlockSpec((tm, D_MODEL), tok),
            pl.BlockSpec((tm, D_MODEL), tok),
            pl.BlockSpec((tm, D_MODEL), tok),
            pl.BlockSpec((None, 3, D_MODEL, D_MODEL), lambda i: (layer, 0, 0, 0), pipeline_mode=once),
            pl.BlockSpec((None, D_MODEL, D_MODEL), lay2, pipeline_mode=once),
            pl.BlockSpec((None, D_MODEL, D_FF), lay2, pipeline_mode=once),
            pl.BlockSpec((None, D_FF, D_MODEL), lay2, pipeline_mode=once),
        ],
        out_specs=pl.BlockSpec((tm, D_MODEL), tok),
        out_shape=jax.ShapeDtypeStruct((n_tok, D_MODEL), F32),
        compiler_params=pltpu.CompilerParams(vmem_limit_bytes=_vmem_limit(60 << 20)),
        name="merge_mlp",
    )(x2d, mod, g1, g2, wg, wg, ya, yb, yc, wb, wo, w1, w2)


def _rope_tables(seq):
    pos = jnp.arange(seq)
    row = (pos // GRID_W).astype(F32)
    col = (pos % GRID_W).astype(F32)
    inv = jnp.power(ROPE_BASE, -jnp.arange(ROPE_FREQS, dtype=F32) / ROPE_FREQS)
    ang_r = row[:, None] * inv
    ang_c = col[:, None] * inv
    cos = jnp.concatenate([jnp.cos(ang_r), jnp.cos(ang_r), jnp.cos(ang_c), jnp.cos(ang_c)], axis=-1)
    sin = jnp.concatenate([-jnp.sin(ang_r), jnp.sin(ang_r), -jnp.sin(ang_c), jnp.sin(ang_c)], axis=-1)
    reps = LANES // HEAD_DIM
    return jnp.tile(cos, (1, reps)), jnp.tile(sin, (1, reps))


def _block_diag(w):
    depth = w.shape[0]
    per = MXU_DIM // RNN_BLOCK
    rows = w.reshape(depth, 2, RNN_BLOCKS // per, MXU_DIM, RNN_BLOCK)
    tiled = jnp.tile(rows, (1, 1, 1, 1, per))
    blk = np.arange(MXU_DIM) // RNN_BLOCK
    return jnp.where(blk[:, None] == blk[None, :], tiled, 0.0).astype(BF16)


def kernel(x, c, ctx, c_ctx, w_mod, b_mod, g_norm1, w_in, conv_w, conv_b, lru_wa, lru_ba, lru_wx, lru_bx,
           lru_lambda, sgu_ln_g, sgu_ln_b, sgu_w, sgu_b, q_norm_g, k_norm_g, sink, w_branch, w_out, g_norm2,
           w_ff1, w_ff2):
    n_batch, n_tok, _ = x.shape
    n_ctx = ctx.shape[1]
    depth = w_mod.shape[0]
    tm_x, tm_c = 512, 512
    assert n_batch + 1 <= MOD_ROWS and n_tok % tm_x == 0 and (n_batch * n_ctx) % tm_c == 0

    cond = jnp.zeros((MOD_ROWS, D_MODEL), F32).at[:n_batch].set(c).at[n_batch].set(c_ctx)
    mod = _modulation(cond, w_mod, b_mod)

    cos, sin = _rope_tables(n_tok)
    q_mul = ATT_SCALE * LOG2E
    tabs_x = (cos * q_mul, sin * q_mul, cos, sin)
    one = jnp.ones((n_batch * n_ctx, LANES), F32)
    zero = jnp.zeros((n_batch * n_ctx, LANES), F32)
    tabs_c = (one * q_mul, zero, one, zero)

    head = np.arange(MXU_DIM) // HEAD_DIM
    ones_bd = jnp.asarray((head[:, None] == head[None, :]) * (1.0 / HEAD_DIM), BF16)

    w_in16 = w_in.astype(BF16)
    wb16 = w_branch.astype(BF16)
    wo16 = w_out.astype(BF16)
    w116 = w_ff1.astype(BF16)
    w216 = w_ff2.astype(BF16)
    ws16 = sgu_w.astype(BF16)
    wa_bd = _block_diag(lru_wa)
    wx_bd = _block_diag(lru_wx)
    ba_h = (0.5 * lru_ba)[:, :, None]
    bx_h = (0.5 * lru_bx)[:, :, None]
    lam = lru_lambda[:, :, None]
    conv_wh = 0.5 * conv_w
    conv_bh = (0.5 * conv_b)[:, None]

    x2d = x.reshape(n_batch * n_tok, D_MODEL)
    cx2d = ctx.reshape(n_batch * n_ctx, D_MODEL)
    h0 = jnp.zeros((2, n_batch, N_SLABS, LANES), F32)
    reps = LANES // HEAD_DIM

    def rglru(layer, xa, init):
        scan_p = (wa_bd, wx_bd, ba_h, bx_h, lam)
        hf, hf_fin, xc = _scan(layer, 0, xa, None, conv_wh, conv_bh, *scan_p, init)
        ya, hb_fin = _scan(layer, 1, xc, hf, None, None, *scan_p, init)
        return ya, jnp.stack([hf_fin, hb_fin])

    g1 = g_norm1[:, None]
    g2 = g_norm2[:, None]
    lng = sgu_ln_g[:, None]
    lnb = sgu_ln_b[:, None]
    bs = jnp.broadcast_to(sgu_b[:, :, :, None], (depth, SGU_GROUPS, SGU_CHUNK, LANES))
    qg = jnp.tile(q_norm_g, (1, reps))[:, None]
    kg = jnp.tile(k_norm_g, (1, reps))[:, None]

    for l in range(depth):
        last = l == depth - 1
        inproj_p = (mod, g1, w_in16, lng, lnb, ws16, bs, qg, kg, ones_bd)
        merge_w = (wb16, wo16, w116, w216)

        xa_c, yb_c, q_c, kd_c, vd_c = _inproj(l, cx2d, *inproj_p, tabs_c, seq=n_batch * n_ctx, mod_base=n_batch,
                                              per_batch=False, tm=tm_c)
        ya_c, hfin_c = rglru(l, xa_c.reshape(n_batch, n_ctx, D_MODEL), h0)
        kd_c = kd_c.reshape(n_batch, n_ctx, -1)
        vd_c = vd_c.reshape(n_batch, n_ctx, -1)

        xa, yb, q, kd, vd = _inproj(l, x2d, *inproj_p, tabs_x, seq=n_tok, mod_base=0, per_batch=True, tm=tm_x)
        ya_x, _ = rglru(l, xa.reshape(n_batch, n_tok, D_MODEL), hfin_c)
        yc = _attention(l, sink, q.reshape(n_batch, n_tok, D_MODEL), kd.reshape(n_batch, n_tok, -1),
                        vd.reshape(n_batch, n_tok, -1), kd_c, vd_c, band=True)
        x2d = _merge_mlp(l, x2d, mod, g1, g2, w_in16, ya_x.reshape(n_batch * n_tok, D_MODEL), yb,
                         yc.reshape(n_batch * n_tok, D_MODEL), *merge_w,
                         mod_base=0, tiles_per_row=n_tok // tm_x, tm=tm_x)

        if not last:
            yc_c = _attention(l, sink, q_c.reshape(n_batch, n_ctx, D_MODEL), None, None, kd_c, vd_c, band=False)
            cx2d = _merge_mlp(l, cx2d, mod, g1, g2, w_in16, ya_c.reshape(n_batch * n_ctx, D_MODEL), yb_c,
                              yc_c.reshape(n_batch * n_ctx, D_MODEL), *merge_w,
                              mod_base=n_batch, tiles_per_row=n_batch * n_ctx // tm_c, tm=tm_c)

    return x2d.reshape(n_batch, n_tok, D_MODEL)
```

```python
import functools

import numpy as np
import jax
import jax.numpy as jnp
from jax import lax
from jax.experimental import pallas as pl
from jax.experimental.pallas import tpu as pltpu

F32 = jnp.float32
BF16 = jnp.bfloat16

D_MODEL = 1024
GRID_W = 64
EPS = 1e-6
RNN_BLOCKS = 16
RNN_BLOCK = D_MODEL // RNN_BLOCKS
CONV_W = 4
CONV_PAD_L = 2
LRU_C = 8.0
SGU_CHUNK = 128
SGU_GROUPS = 8
N_HEADS = 16
N_KV_HEADS = 4
HEAD_DIM = 64
WINDOW = 128
ATT_BLOCK = 128
ATT_SCALE = HEAD_DIM ** -0.5
ROPE_BASE = 10000.0
ROPE_FREQS = HEAD_DIM // 4
D_FF = 4 * D_MODEL
OFF_B = D_MODEL
OFF_Q = OFF_B + 2 * D_MODEL
OFF_K = OFF_Q + N_HEADS * HEAD_DIM
OFF_V = OFF_K + N_KV_HEADS * HEAD_DIM
OFF_G = OFF_V + N_KV_HEADS * HEAD_DIM

LANES = 128
SUBLANES = 8
MXU_DIM = 256
VMEM_BYTES = 64 * 1024 * 1024

N_SLABS = D_MODEL // LANES
MOD_ROWS = SUBLANES
NEG_BIG = -1e30
LOG2E = 1.4426950408889634
LN2 = 0.6931471805599453
SCAN_GROUP = 16
GATE_BLK = 1536
MERGE_SUB = 256
FF_CHUNKS = 2
ATT_QB = 8
ATT_LOOKAHEAD = 2
ATT_VALUE_LAG = 2


def _vmem_limit(nbytes):
    return int(min(nbytes, VMEM_BYTES - 4 * 1024 * 1024))


def _rms(x, g):
    ms = jnp.mean(x * x, axis=-1, keepdims=True)
    return x * lax.rsqrt(ms + EPS) * g


def _gelu_tanh(x):
    c = np.sqrt(2.0 / np.pi).astype(np.float32)
    inner = x * (c + (0.044715 * c) * (x * x))
    hx = 0.5 * x
    return hx + hx * jnp.tanh(inner)


def _dot(a, b):
    return jnp.dot(a, b, preferred_element_type=F32)


def _mod_kernel(c_ref, w_ref, b_ref, o_ref):
    c = c_ref[...]
    s = c * jax.nn.sigmoid(c)
    o_ref[0] = _dot(s.astype(BF16), w_ref[0].astype(BF16)) + b_ref[0]


def _modulation(cond, w_mod, b_mod):
    depth, _, width = w_mod.shape
    tn = 1536
    return pl.pallas_call(
        _mod_kernel,
        grid=(depth, width // tn),
        in_specs=[
            pl.BlockSpec((MOD_ROWS, D_MODEL), lambda l, j: (0, 0)),
            pl.BlockSpec((1, D_MODEL, tn), lambda l, j: (l, 0, j)),
            pl.BlockSpec((1, 1, tn), lambda l, j: (l, 0, j)),
        ],
        out_specs=pl.BlockSpec((1, MOD_ROWS, tn), lambda l, j: (l, 0, j)),
        out_shape=jax.ShapeDtypeStruct((depth, MOD_ROWS, width), F32),
        compiler_params=pltpu.CompilerParams(vmem_limit_bytes=_vmem_limit(40 << 20)),
        name="modulation",
    )(cond, w_mod, b_mod.reshape(depth, 1, width))


def _inproj_kernel(x_ref, mod_ref, g1_ref, w_ref, lng_ref, lnb_ref, ws_ref, bs_ref, qg_ref, kg_ref,
                   ones_ref, cq_ref, sq_ref, ck_ref, sk_ref,
                   xa_ref, yb_ref, q_ref, kd_ref, vt_ref, *, tm, mod_base, tiles_per_row):
    i = pl.program_id(0)
    row = mod_base + i // tiles_per_row
    shift = mod_ref[pl.ds(row, 1), 0:D_MODEL]
    scale = mod_ref[pl.ds(row, 1), D_MODEL:2 * D_MODEL]
    h = (_rms(x_ref[...], g1_ref[...] * (1.0 + scale)) + shift).astype(BF16)

    v_raw = _dot(h, w_ref[:, OFF_B + D_MODEL:OFF_Q])
    u_raw = _dot(h, w_ref[:, OFF_B:OFF_B + D_MODEL])
    q_raw = _dot(h, w_ref[:, OFF_Q:OFF_K])

    v = _gelu_tanh(v_raw)
    u = _gelu_tanh(u_raw)
    mu = jnp.mean(v, axis=-1, keepdims=True)
    vc = v - mu
    var = jnp.mean(vc * vc, axis=-1, keepdims=True)
    vn = (vc * lax.rsqrt(var + EPS) * lng_ref[...] + lnb_ref[...]).astype(BF16)
    n_chunks = tm // SGU_CHUNK
    for g in range(SGU_GROUPS):
        cols = slice(g * LANES, (g + 1) * LANES)
        rhs = jnp.concatenate([vn[c * SGU_CHUNK:(c + 1) * SGU_CHUNK, cols] for c in range(n_chunks)], axis=1)
        mixed = _dot(ws_ref[g], rhs)
        for c in range(n_chunks):
            rows = slice(c * SGU_CHUNK, (c + 1) * SGU_CHUNK)
            yb_ref[rows, cols] = (u[rows, cols] * (mixed[:, c * LANES:(c + 1) * LANES] + bs_ref[g])).astype(BF16)

    k_raw = _dot(h, w_ref[:, OFF_K:OFF_V])
    vv = _dot(h, w_ref[:, OFF_V:OFF_G])

    lane = lax.broadcasted_iota(jnp.int32, (tm, LANES), 1)
    first_half = (lane % (2 * ROPE_FREQS)) < ROPE_FREQS
    low_head = lane < HEAD_DIM

    def head_norm_rope(z, g_ref, c_ref, s_ref, out_ref):
        zz = (z * z).astype(BF16)
        for blk in range(z.shape[1] // MXU_DIM):
            cols = slice(blk * MXU_DIM, (blk + 1) * MXU_DIM)
            ms = _dot(zz[:, cols], ones_ref[...])
            zn = z[:, cols] * lax.rsqrt(ms + EPS)
            for s in range(MXU_DIM // LANES):
                t = zn[:, s * LANES:(s + 1) * LANES] * g_ref[...]
                sw = jnp.where(first_half, pltpu.roll(t, LANES - ROPE_FREQS, 1),
                               pltpu.roll(t, ROPE_FREQS, 1))
                slab = blk * (MXU_DIM // LANES) + s
                out_ref(slab, t * c_ref[...] + sw * s_ref[...])

    def store_q(slab, val):
        q_ref[:, slab * LANES:(slab + 1) * LANES] = val.astype(BF16)

    def dup_heads(slab_val):
        r = pltpu.roll(slab_val, HEAD_DIM, 1)
        return jnp.where(low_head, slab_val, r), jnp.where(low_head, r, slab_val)

    def store_kd(slab, val):
        a, b = dup_heads(val)
        kd_ref[:, (2 * slab) * LANES:(2 * slab + 1) * LANES] = a.astype(BF16)
        kd_ref[:, (2 * slab + 1) * LANES:(2 * slab + 2) * LANES] = b.astype(BF16)

    head_norm_rope(q_raw, qg_ref, cq_ref, sq_ref, store_q)
    head_norm_rope(k_raw, kg_ref, ck_ref, sk_ref, store_kd)

    vt = vv.T.astype(BF16)
    ones_rows = jnp.ones((HEAD_DIM, tm), BF16)
    for kh in range(N_KV_HEADS):
        vt_ref[(2 * kh) * HEAD_DIM:(2 * kh + 1) * HEAD_DIM, :] = vt[kh * HEAD_DIM:(kh + 1) * HEAD_DIM, :]
        vt_ref[(2 * kh + 1) * HEAD_DIM:(2 * kh + 2) * HEAD_DIM, :] = ones_rows

    xa_ref[...] = _dot(h, w_ref[:, 0:OFF_B])


def _inproj(layer, x2d, mod, g1, w_a, lng, lnb, ws, bs, qg, kg, ones_bd, tabs, *, seq, mod_base, per_batch, tm):
    n_tok = x2d.shape[0]
    tiles_per_seq = seq // tm
    tiles_per_row = tiles_per_seq if per_batch else n_tok // tm
    cq, sq, ck, sk = tabs
    const = lambda i: (0, 0)
    lay2 = lambda i: (layer, 0, 0)
    tab_map = lambda i: (i % tiles_per_seq, 0)
    kvw = 2 * N_KV_HEADS * HEAD_DIM
    return pl.pallas_call(
        functools.partial(_inproj_kernel, tm=tm, mod_base=mod_base, tiles_per_row=tiles_per_row),
        grid=(n_tok // tm,),
        in_specs=[
            pl.BlockSpec((tm, D_MODEL), lambda i: (i, 0)),
            pl.BlockSpec((None, MOD_ROWS, 6 * D_MODEL), lay2),
            pl.BlockSpec((None, 1, D_MODEL), lay2),
            pl.BlockSpec((None, D_MODEL, OFF_G), lay2, pipeline_mode=pl.Buffered(1)),
            pl.BlockSpec((None, 1, D_MODEL), lay2),
            pl.BlockSpec((None, 1, D_MODEL), lay2),
            pl.BlockSpec((None, SGU_GROUPS, SGU_CHUNK, SGU_CHUNK), lambda i: (layer, 0, 0, 0)),
            pl.BlockSpec((None, SGU_GROUPS, SGU_CHUNK, LANES), lambda i: (layer, 0, 0, 0)),
            pl.BlockSpec((None, 1, LANES), lay2),
            pl.BlockSpec((None, 1, LANES), lay2),
            pl.BlockSpec((MXU_DIM, MXU_DIM), const),
            pl.BlockSpec((tm, LANES), tab_map),
            pl.BlockSpec((tm, LANES), tab_map),
            pl.BlockSpec((tm, LANES), tab_map),
            pl.BlockSpec((tm, LANES), tab_map),
        ],
        out_specs=[
            pl.BlockSpec((tm, D_MODEL), lambda i: (i, 0)),
            pl.BlockSpec((tm, D_MODEL), lambda i: (i, 0)),
            pl.BlockSpec((tm, D_MODEL), lambda i: (i, 0)),
            pl.BlockSpec((tm, kvw), lambda i: (i, 0)),
            pl.BlockSpec((kvw, tm), lambda i: (0, i)),
        ],
        out_shape=[
            jax.ShapeDtypeStruct((n_tok, D_MODEL), F32),
            jax.ShapeDtypeStruct((n_tok, D_MODEL), BF16),
            jax.ShapeDtypeStruct((n_tok, D_MODEL), BF16),
            jax.ShapeDtypeStruct((n_tok, kvw), BF16),
            jax.ShapeDtypeStruct((kvw, n_tok), BF16),
        ],
        compiler_params=pltpu.CompilerParams(vmem_limit_bytes=_vmem_limit(56 << 20)),
        name="inproj",
    )(x2d, mod, g1, w_a, lng, lnb, ws, bs, qg, kg, ones_bd, cq, sq, ck, sk)


def _scan_kernel(*refs, tc, nchunk, nb, reverse, conv):
    if conv:
        (xa_ref, xp_ref, xn_ref, cw_ref, cb_ref, wa_ref, wx_ref, ba_ref, bx_ref, lam_ref, h0_ref,
         out_ref, hfin_ref, xc_ref, xe_ref, a_ref, u_ref, h_ref, hc_ref) = refs
    else:
        (xc_ref, hf_ref, wa_ref, wx_ref, ba_ref, bx_ref, lam_ref, h0_ref,
         out_ref, hfin_ref, a_ref, u_ref, h_ref, hc_ref) = refs
    i = pl.program_id(0)
    c = (nchunk - 1 - i) if reverse else i

    @pl.when(i == 0)
    def _():
        hc_ref[...] = h0_ref[...]

    z = -lam_ref[...]
    softplus = jnp.maximum(z, 0.0) + jnp.log1p(jnp.exp(-jnp.abs(z)))
    nc2l = (-0.5 * LRU_C * LOG2E) * softplus

    for b in range(nb):
        if conv:
            xe_ref[0:SUBLANES, :] = xp_ref[b] * (c > 0).astype(F32)
            xe_ref[SUBLANES:SUBLANES + tc, :] = xa_ref[b]
            xe_ref[SUBLANES + tc:2 * SUBLANES + tc, :] = xn_ref[b] * (c < nchunk - 1).astype(F32)
            xe = xe_ref[...]
            n_rows = tc + 2 * SUBLANES
            xc = cb_ref[...]
            for k in range(CONV_W):
                back = CONV_PAD_L - k
                xs = xe if back == 0 else pltpu.roll(xe, back % n_rows, 0)
                xc = xc + cw_ref[k:k + 1, :] * xs[SUBLANES:SUBLANES + tc]
            xc_ref[b] = xc
        else:
            xc = xc_ref[b]
        for blk in range(D_MODEL // MXU_DIM):
            cols = slice(blk * MXU_DIM, (blk + 1) * MXU_DIM)
            xcb = xc[:, cols]
            xcb16 = xcb.astype(BF16)
            t_r = jnp.tanh(_dot(xcb16, wa_ref[blk]) + ba_ref[:, cols])
            t_i = jnp.tanh(_dot(xcb16, wx_ref[blk]) + bx_ref[:, cols])
            log2_a = nc2l[:, cols] * t_r + nc2l[:, cols]
            a = jnp.exp2(log2_a)
            s = jnp.tanh(log2_a * (-LN2)) * (a * a + 1.0)
            root = jnp.where(s > 0.0, s * lax.rsqrt(s), 0.0)
            u = xcb * (t_i + 1.0) * root
            for sl in range(MXU_DIM // LANES):
                slab = blk * (MXU_DIM // LANES) + sl
                a_ref[b, pl.ds(slab, tc, stride=N_SLABS), :] = a[:, sl * LANES:(sl + 1) * LANES]
                u_ref[b, pl.ds(slab, tc, stride=N_SLABS), :] = u[:, sl * LANES:(sl + 1) * LANES]

    n_groups = tc // SCAN_GROUP

    def group(g, hs):
        gg = (n_groups - 1 - g) if reverse else g
        base = pl.multiple_of(gg * (SCAN_GROUP * N_SLABS), SCAN_GROUP * N_SLABS)
        hs = list(hs)
        for j in range(SCAN_GROUP):
            rows = pl.ds(base + ((SCAN_GROUP - 1 - j) if reverse else j) * N_SLABS, N_SLABS)
            for b in range(nb):
                hs[b] = a_ref[b, rows, :] * hs[b] + u_ref[b, rows, :]
                h_ref[b, rows, :] = hs[b]
        return tuple(hs)

    hs = lax.fori_loop(0, n_groups, group, tuple(hc_ref[b] for b in range(nb)))
    for b in range(nb):
        hc_ref[b] = hs[b]
        hfin_ref[b] = hs[b]
        for slab in range(N_SLABS):
            cols = slice(slab * LANES, (slab + 1) * LANES)
            h_tok = h_ref[b, pl.ds(slab, tc, stride=N_SLABS), :]
            if not conv:
                h_tok = h_tok + hf_ref[b, :, cols].astype(F32)
            out_ref[b, :, cols] = h_tok.astype(BF16)


def _scan(layer, direction, x_in, hf, cw, cb, wa_bd, wx_bd, ba, bx, lam, h0):
    nb, seq, _ = x_in.shape
    tc = 256
    nchunk = seq // tc
    conv = direction == 0
    reverse = direction == 1
    chunk = (lambda i: nchunk - 1 - i) if reverse else (lambda i: i)
    nblk = D_MODEL // MXU_DIM
    tok_spec = pl.BlockSpec((nb, tc, D_MODEL), lambda i: (0, chunk(i), 0))
    gate_w = pl.BlockSpec((None, None, nblk, MXU_DIM, MXU_DIM), lambda i: (layer, direction, 0, 0, 0))
    vec = pl.BlockSpec((None, None, 1, D_MODEL), lambda i: (layer, direction, 0, 0))
    state = pl.BlockSpec((None, nb, N_SLABS, LANES), lambda i: (direction, 0, 0, 0))
    in_specs = [tok_spec]
    args = [x_in]
    if conv:
        halo_blocks = seq // SUBLANES
        per = tc // SUBLANES
        in_specs += [
            pl.BlockSpec((nb, SUBLANES, D_MODEL), lambda i: (0, jnp.maximum(i * per - 1, 0), 0)),
            pl.BlockSpec((nb, SUBLANES, D_MODEL), lambda i: (0, jnp.minimum((i + 1) * per, halo_blocks - 1), 0)),
            pl.BlockSpec((None, CONV_W, D_MODEL), lambda i: (layer, 0, 0)),
            pl.BlockSpec((None, 1, D_MODEL), lambda i: (layer, 0, 0)),
        ]
        args += [x_in, x_in, cw, cb]
    else:
        in_specs.append(tok_spec)
        args.append(hf)
    in_specs += [gate_w, gate_w, vec, vec, vec, state]
    args += [wa_bd, wx_bd, ba, bx, lam, h0]
    out_specs = [
        pl.BlockSpec((nb, tc, D_MODEL), lambda i: (0, chunk(i), 0)),
        pl.BlockSpec((nb, N_SLABS, LANES), lambda i: (0, 0, 0)),
    ]
    out_shape = [
        jax.ShapeDtypeStruct((nb, seq, D_MODEL), BF16),
        jax.ShapeDtypeStruct((nb, N_SLABS, LANES), F32),
    ]
    scratch = []
    if conv:
        out_specs.append(pl.BlockSpec((nb, tc, D_MODEL), lambda i: (0, i, 0)))
        out_shape.append(jax.ShapeDtypeStruct((nb, seq, D_MODEL), F32))
        scratch.append(pltpu.VMEM((tc + 2 * SUBLANES, D_MODEL), F32))
    scratch += [pltpu.VMEM((nb, tc * N_SLABS, LANES), F32)] * 3 + [pltpu.VMEM((nb, N_SLABS, LANES), F32)]
    return pl.pallas_call(
        functools.partial(_scan_kernel, tc=tc, nchunk=nchunk, nb=nb, reverse=reverse, conv=conv),
        grid=(nchunk,),
        in_specs=in_specs,
        out_specs=out_specs,
        out_shape=out_shape,
        scratch_shapes=scratch,
        compiler_params=pltpu.CompilerParams(vmem_limit_bytes=_vmem_limit(56 << 20)),
        name="scan_fwd" if conv else "scan_bwd",
    )(*args)


def _attn_kernel(sink_ref, q_ref, *refs, layer, n_steps, qb, band):
    if band:
        kp_ref, kcur_ref, kn_ref, vp_ref, vcur_ref, vn_ref, kx_ref, vx_ref, o_ref = refs
    else:
        kx_ref, vx_ref, o_ref = refs
    n = pl.program_id(1)
    lane = lax.broadcasted_iota(jnp.int32, (1, LANES), 1)
    low = lane < HEAD_DIM
    high = jnp.logical_not(low)

    def block(j):
        return slice(j * ATT_BLOCK, (j + 1) * ATT_BLOCK)

    if band:
        kj = lax.broadcasted_iota(jnp.int32, (ATT_BLOCK, ATT_BLOCK), 0)
        qi = lax.broadcasted_iota(jnp.int32, (ATT_BLOCK, ATT_BLOCK), 1)

        def tiled(visible):
            one = jnp.where(visible, 0.0, NEG_BIG).astype(F32)
            return jnp.concatenate([one, one], axis=1)

        bias_prev = [tiled((kj >= qi) & (n > 0)) if j == 0 else tiled(kj >= qi) for j in range(qb)]
        bias_next = [tiled((kj <= qi) & (n < n_steps - 1)) if j == qb - 1 else tiled(kj <= qi) for j in range(qb)]

    def split(chain):
        j, rest = divmod(chain, 2 * N_KV_HEADS)
        kh, half = divmod(rest, 2)
        return j, kh, half

    def scores(chain):
        j, kh, half = split(chain)
        q2 = jnp.concatenate([q_ref[block(j), (2 * kh) * LANES:(2 * kh + 1) * LANES],
                              q_ref[block(j), (2 * kh + 1) * LANES:(2 * kh + 2) * LANES]], axis=0)
        q2 = jnp.where(low if half == 0 else high, q2, jnp.zeros_like(q2))
        slab = slice(kh * LANES, (kh + 1) * LANES)
        if band:
            blocks = [kp_ref[:, slab]] + [kcur_ref[block(i), slab] for i in range(qb)] + [kn_ref[:, slab]]
            kall = jnp.concatenate(blocks[j:j + 3] + [kx_ref[:, slab]], axis=0)
        else:
            kall = kx_ref[:, slab]
        return lax.dot_general(kall, q2, (((1,), (1,)), ((), ())), preferred_element_type=F32)

    def weights(chain, logits):
        j, kh, half = split(chain)
        pieces = [logits[i * ATT_BLOCK:(i + 1) * ATT_BLOCK, :] for i in range(logits.shape[0] // ATT_BLOCK)]
        if band:
            pieces[0] = pieces[0] + bias_prev[j]
            pieces[2] = pieces[2] + bias_next[j]
        h_top = 4 * kh + half
        sk = jnp.concatenate([jnp.full((1, LANES), sink_ref[layer, h_top] * LOG2E, F32),
                              jnp.full((1, LANES), sink_ref[layer, h_top + 2] * LOG2E, F32)], axis=1)
        widest = pieces[0]
        for piece in pieces[1:]:
            widest = jnp.maximum(widest, piece)
        m = jnp.maximum(sk, jnp.max(widest, axis=0, keepdims=True))
        p = jnp.concatenate([jnp.exp2(piece - m).astype(BF16) for piece in pieces], axis=0)
        return p, jnp.exp2(sk - m)

    def values(chain, p):
        j, kh, half = split(chain)
        rows = slice(kh * LANES, (kh + 1) * LANES)
        if band:
            blocks = [vp_ref[rows, :]] + [vcur_ref[rows, block(i)] for i in range(qb)] + [vn_ref[rows, :]]
            vall = jnp.concatenate(blocks[j:j + 3] + [vx_ref[rows, :]], axis=1)
        else:
            vall = vx_ref[rows, :]
        return _dot(vall, p)

    def finish(j, kh, pv, sink_p):
        res = [pv[h][:HEAD_DIM, :] / (pv[h][HEAD_DIM:HEAD_DIM + 1, :] + sink_p[h]) for h in range(2)]
        for i in range(2):
            cols = slice(i * ATT_BLOCK, (i + 1) * ATT_BLOCK)
            both = jnp.concatenate([res[0][:, cols], res[1][:, cols]], axis=0)
            o_ref[block(j), (2 * kh + i) * LANES:(2 * kh + i + 1) * LANES] = both.T.astype(BF16)

    n_chains = qb * 2 * N_KV_HEADS
    logits = {c: scores(c) for c in range(min(ATT_LOOKAHEAD, n_chains))}
    probs, sink_p, pv = {}, {}, {}
    for step in range(n_chains + ATT_VALUE_LAG):
        if step + ATT_LOOKAHEAD < n_chains:
            logits[step + ATT_LOOKAHEAD] = scores(step + ATT_LOOKAHEAD)
        if step < n_chains:
            probs[step], sink_p[step] = weights(step, logits.pop(step))
        c = step - ATT_VALUE_LAG
        if c >= 0:
            pv[c] = values(c, probs.pop(c))
            if c % 2 == 1:
                j, kh, _ = split(c)
                finish(j, kh, [pv.pop(c - 1), pv.pop(c)], [sink_p.pop(c - 1), sink_p.pop(c)])


def _attention(layer, sink, q, kd, vt, kdx, vtx, *, band, qb=ATT_QB):
    nb, seq, _ = q.shape
    nctx = kdx.shape[1]
    nblk = seq // ATT_BLOCK
    qb = min(qb, nblk)
    assert nblk % qb == 0
    n_steps = nblk // qb
    kvw = kdx.shape[-1]
    qspec = pl.BlockSpec((None, qb * ATT_BLOCK, D_MODEL), lambda b, n: (b, n, 0))
    in_specs = [pl.BlockSpec(memory_space=pltpu.SMEM), qspec]
    args = [sink, q]
    if band:
        first = lambda b, n: jnp.maximum(n * qb - 1, 0)
        last = lambda b, n: jnp.minimum((n + 1) * qb, nblk - 1)
        in_specs += [
            pl.BlockSpec((None, ATT_BLOCK, kvw), lambda b, n: (b, first(b, n), 0)),
            pl.BlockSpec((None, qb * ATT_BLOCK, kvw), lambda b, n: (b, n, 0)),
            pl.BlockSpec((None, ATT_BLOCK, kvw), lambda b, n: (b, last(b, n), 0)),
            pl.BlockSpec((kvw, ATT_BLOCK), lambda b, n: (0, b * nblk + first(b, n))),
            pl.BlockSpec((kvw, qb * ATT_BLOCK), lambda b, n: (0, b * n_steps + n)),
            pl.BlockSpec((kvw, ATT_BLOCK), lambda b, n: (0, b * nblk + last(b, n))),
        ]
        args += [kd, kd, kd, vt, vt, vt]
    in_specs += [pl.BlockSpec((None, nctx, kvw), lambda b, n: (b, 0, 0)),
                 pl.BlockSpec((kvw, nctx), lambda b, n: (0, b))]
    args += [kdx, vtx]
    return pl.pallas_call(
        functools.partial(_attn_kernel, layer=layer, n_steps=n_steps, qb=qb, band=band),
        grid=(nb, n_steps),
        in_specs=in_specs,
        out_specs=pl.BlockSpec((None, qb * ATT_BLOCK, D_MODEL), lambda b, n: (b, n, 0)),
        out_shape=jax.ShapeDtypeStruct((nb, seq, D_MODEL), BF16),
        compiler_params=pltpu.CompilerParams(vmem_limit_bytes=_vmem_limit(40 << 20)),
        name="attention_band" if band else "attention_ctx",
    )(*args)


def _merge_mlp_kernel(x_ref, mod_ref, g1_ref, g2_ref, wga_ref, wgb_ref, ya_ref, yb_ref, yc_ref, wb_ref, wo_ref,
                      w1_ref, w2_ref, o_ref, *, mod_base, tiles_per_row):
    i = pl.program_id(0)
    row = mod_base + i // tiles_per_row

    def mod(k):
        return mod_ref[pl.ds(row, 1), k * D_MODEL:(k + 1) * D_MODEL]

    def gate_logits(h1, k):
        parts = []
        for ref, base in ((wga_ref, 0), (wgb_ref, GATE_BLK)):
            lo = max(k * D_MODEL, base) - base
            hi = min((k + 1) * D_MODEL, base + GATE_BLK) - base
            if hi > lo:
                parts.append(_dot(h1, ref[:, lo:hi]))
        return parts[0] if len(parts) == 1 else jnp.concatenate(parts, axis=1)

    subs = [slice(j * MERGE_SUB, (j + 1) * MERGE_SUB) for j in range(x_ref.shape[0] // MERGE_SUB)]
    xs = [x_ref[r, :] for r in subs]
    ms = []
    for j, r in enumerate(subs):
        h1 = (_rms(xs[j], g1_ref[...] * (1.0 + mod(1))) + mod(0)).astype(BF16)
        branches = (ya_ref[r, :], yb_ref[r, :], yc_ref[r, :])
        m = None
        for k in range(3):
            gate = jax.nn.sigmoid(gate_logits(h1, k))
            term = gate * _dot(branches[k], wb_ref[k])
            m = term if m is None else m + term
        ms.append(m.astype(BF16))
    x1s = [xs[j] + mod(2) * _dot(ms[j], wo_ref[...]) for j in range(len(subs))]
    h2s = [(_rms(x1, g2_ref[...] * (1.0 + mod(4))) + mod(3)).astype(BF16) for x1 in x1s]
    acc = list(x1s)
    ff = D_FF // FF_CHUNKS
    for c in range(FF_CHUNKS):
        fs = [jnp.maximum(_dot(h2, w1_ref[:, c * ff:(c + 1) * ff]), 0.0) for h2 in h2s]
        for j in range(len(subs)):
            acc[j] = acc[j] + mod(5) * _dot((fs[j] * fs[j]).astype(BF16), w2_ref[c * ff:(c + 1) * ff, :])
    for j, r in enumerate(subs):
        o_ref[r, :] = acc[j]


def _merge_mlp(layer, x2d, mod, g1, g2, wg, ya, yb, yc, wb, wo, w1, w2, *, mod_base, tiles_per_row, tm):
    n_tok = x2d.shape[0]
    const = lambda i: (0, 0)
    lay2 = lambda i: (layer, 0, 0)
    once = pl.Buffered(1)
    tok = lambda i: (i, 0)
    return pl.pallas_call(
        functools.partial(_merge_mlp_kernel, mod_base=mod_base, tiles_per_row=tiles_per_row),
        grid=(n_tok // tm,),
        in_specs=[
            pl.BlockSpec((tm, D_MODEL), tok),
            pl.BlockSpec((None, MOD_ROWS, 6 * D_MODEL), lay2),
            pl.BlockSpec((None, 1, D_MODEL), lay2),
            pl.BlockSpec((None, 1, D_MODEL), lay2),
            pl.BlockSpec((None, D_MODEL, GATE_BLK), lambda i: (layer, 0, OFF_G // GATE_BLK), pipeline_mode=once),
            pl.BlockSpec((None, D_MODEL, GATE_BLK), lambda i: (layer, 0, OFF_G // GATE_BLK + 1), pipeline_mode=once),
            pl.BlockSpec((tm, D_MODEL), tok),
            pl.BlockSpec((tm, D_MODEL), tok),
            pl.BlockSpec((tm, D_MODEL), tok),
            pl.BlockSpec((None, 3, D_MODEL, D_MODEL), lambda i: (layer, 0, 0, 0), pipeline_mode=once),
            pl.BlockSpec((None, D_MODEL, D_MODEL), lay2, pipeline_mode=once),
            pl.BlockSpec((None, D_MODEL, D_FF), lay2, pipeline_mode=once),
            pl.BlockSpec((None, D_FF, D_MODEL), lay2, pipeline_mode=once),
        ],
        out_specs=pl.BlockSpec((tm, D_MODEL), tok),
        out_shape=jax.ShapeDtypeStruct((n_tok, D_MODEL), F32),
        compiler_params=pltpu.CompilerParams(vmem_limit_bytes=_vmem_limit(60 << 20)),
        name="merge_mlp",
    )(x2d, mod, g1, g2, wg, wg, ya, yb, yc, wb, wo, w1, w2)


def _rope_tables(seq):
    pos = jnp.arange(seq)
    row = (pos // GRID_W).astype(F32)
    col = (pos % GRID_W).astype(F32)
    inv = jnp.power(ROPE_BASE, -jnp.arange(ROPE_FREQS, dtype=F32) / ROPE_FREQS)
    ang_r = row[:, None] * inv
    ang_c = col[:, None] * inv
    cos = jnp.concatenate([jnp.cos(ang_r), jnp.cos(ang_r), jnp.cos(ang_c), jnp.cos(ang_c)], axis=-1)
    sin = jnp.concatenate([-jnp.sin(ang_r), jnp.sin(ang_r), -jnp.sin(ang_c), jnp.sin(ang_c)], axis=-1)
    reps = LANES // HEAD_DIM
    return jnp.tile(cos, (1, reps)), jnp.tile(sin, (1, reps))


def _block_diag(w):
    depth = w.shape[0]
    per = MXU_DIM // RNN_BLOCK
    rows = w.reshape(depth, 2, RNN_BLOCKS // per, MXU_DIM, RNN_BLOCK)
    tiled = jnp.tile(rows, (1, 1, 1, 1, per))
    blk = np.arange(MXU_DIM) // RNN_BLOCK
    return jnp.where(blk[:, None] == blk[None, :], tiled, 0.0).astype(BF16)


def kernel(x, c, ctx, c_ctx, w_mod, b_mod, g_norm1, w_in, conv_w, conv_b, lru_wa, lru_ba, lru_wx, lru_bx,
           lru_lambda, sgu_ln_g, sgu_ln_b, sgu_w, sgu_b, q_norm_g, k_norm_g, sink, w_branch, w_out, g_norm2,
           w_ff1, w_ff2):
    n_batch, n_tok, _ = x.shape
    n_ctx = ctx.shape[1]
    depth = w_mod.shape[0]
    tm_x, tm_c = 512, 512
    assert n_batch + 1 <= MOD_ROWS and n_tok % tm_x == 0 and (n_batch * n_ctx) % tm_c == 0

    cond = jnp.zeros((MOD_ROWS, D_MODEL), F32).at[:n_batch].set(c).at[n_batch].set(c_ctx)
    mod = _modulation(cond, w_mod, b_mod)

    cos, sin = _rope_tables(n_tok)
    q_mul = ATT_SCALE * LOG2E
    tabs_x = (cos * q_mul, sin * q_mul, cos, sin)
    one = jnp.ones((n_batch * n_ctx, LANES), F32)
    zero = jnp.zeros((n_batch * n_ctx, LANES), F32)
    tabs_c = (one * q_mul, zero, one, zero)

    head = np.arange(MXU_DIM) // HEAD_DIM
    ones_bd = jnp.asarray((head[:, None] == head[None, :]) * (1.0 / HEAD_DIM), BF16)

    w_in16 = w_in.astype(BF16)
    wb16 = w_branch.astype(BF16)
    wo16 = w_out.astype(BF16)
    w116 = w_ff1.astype(BF16)
    w216 = w_ff2.astype(BF16)
    ws16 = sgu_w.astype(BF16)
    wa_bd = _block_diag(lru_wa)
    wx_bd = _block_diag(lru_wx)
    ba_h = (0.5 * lru_ba)[:, :, None]
    bx_h = (0.5 * lru_bx)[:, :, None]
    lam = lru_lambda[:, :, None]
    conv_wh = 0.5 * conv_w
    conv_bh = (0.5 * conv_b)[:, None]

    x2d = x.reshape(n_batch * n_tok, D_MODEL)
    cx2d = ctx.reshape(n_batch * n_ctx, D_MODEL)
    h0 = jnp.zeros((2, n_batch, N_SLABS, LANES), F32)
    reps = LANES // HEAD_DIM

    def rglru(layer, xa, init):
        scan_p = (wa_bd, wx_bd, ba_h, bx_h, lam)
        hf, hf_fin, xc = _scan(layer, 0, xa, None, conv_wh, conv_bh, *scan_p, init)
        ya, hb_fin = _scan(layer, 1, xc, hf, None, None, *scan_p, init)
        return ya, jnp.stack([hf_fin, hb_fin])

    g1 = g_norm1[:, None]
    g2 = g_norm2[:, None]
    lng = sgu_ln_g[:, None]
    lnb = sgu_ln_b[:, None]
    bs = jnp.broadcast_to(sgu_b[:, :, :, None], (depth, SGU_GROUPS, SGU_CHUNK, LANES))
    qg = jnp.tile(q_norm_g, (1, reps))[:, None]
    kg = jnp.tile(k_norm_g, (1, reps))[:, None]

    for l in range(depth):
        last = l == depth - 1
        inproj_p = (mod, g1, w_in16, lng, lnb, ws16, bs, qg, kg, ones_bd)
        merge_w = (wb16, wo16, w116, w216)

        xa_c, yb_c, q_c, kd_c, vt_c = _inproj(l, cx2d, *inproj_p, tabs_c, seq=n_batch * n_ctx, mod_base=n_batch,
                                              per_batch=False, tm=tm_c)
        ya_c, hfin_c = rglru(l, xa_c.reshape(n_batch, n_ctx, D_MODEL), h0)
        kd_c = kd_c.reshape(n_batch, n_ctx, -1)

        xa, yb, q, kd, vt = _inproj(l, x2d, *inproj_p, tabs_x, seq=n_tok, mod_base=0, per_batch=True, tm=tm_x)
        ya_x, _ = rglru(l, xa.reshape(n_batch, n_tok, D_MODEL), hfin_c)
        yc = _attention(l, sink, q.reshape(n_batch, n_tok, D_MODEL), kd.reshape(n_batch, n_tok, -1), vt,
                        kd_c, vt_c, band=True)
        x2d = _merge_mlp(l, x2d, mod, g1, g2, w_in16, ya_x.reshape(n_batch * n_tok, D_MODEL), yb,
                         yc.reshape(n_batch * n_tok, D_MODEL), *merge_w,
                         mod_base=0, tiles_per_row=n_tok // tm_x, tm=tm_x)

        if not last:
            yc_c = _attention(l, sink, q_c.reshape(n_batch, n_ctx, D_MODEL), None, None, kd_c, vt_c, band=False)
            cx2d = _merge_mlp(l, cx2d, mod, g1, g2, w_in16, ya_c.reshape(n_batch * n_ctx, D_MODEL), yb_c,
                              yc_c.reshape(n_batch * n_ctx, D_MODEL), *merge_w,
                              mod_base=n_batch, tiles_per_row=n_batch * n_ctx // tm_c, tm=tm_c)

    return x2d.reshape(n_batch, n_tok, D_MODEL)
```

```python
import functools

import numpy as np
import jax
import jax.numpy as jnp
from jax import lax
from jax.experimental import pallas as pl
from jax.experimental.pallas import tpu as pltpu

F32 = jnp.float32
BF16 = jnp.bfloat16

D_MODEL = 1024
GRID_W = 64
EPS = 1e-6
RNN_BLOCKS = 16
RNN_BLOCK = D_MODEL // RNN_BLOCKS
CONV_W = 4
CONV_PAD_L = 2
LRU_C = 8.0
SGU_CHUNK = 128
SGU_GROUPS = 8
N_HEADS = 16
N_KV_HEADS = 4
HEAD_DIM = 64
WINDOW = 128
ATT_BLOCK = 128
ATT_SCALE = HEAD_DIM ** -0.5
ROPE_BASE = 10000.0
ROPE_FREQS = HEAD_DIM // 4
D_FF = 4 * D_MODEL
OFF_B = D_MODEL
OFF_Q = OFF_B + 2 * D_MODEL
OFF_K = OFF_Q + N_HEADS * HEAD_DIM
OFF_V = OFF_K + N_KV_HEADS * HEAD_DIM
OFF_G = OFF_V + N_KV_HEADS * HEAD_DIM

LANES = 128
SUBLANES = 8
MXU_DIM = 256
VMEM_BYTES = 64 * 1024 * 1024

N_SLABS = D_MODEL // LANES
MOD_ROWS = SUBLANES
NEG_BIG = -1e30
LOG2E = 1.4426950408889634
LN2 = 0.6931471805599453
SCAN_GROUP = 16
GATE_BLK = 1536
MERGE_SUB = 256
FF_CHUNKS = 2
ATT_QB = 8
VT_ROWS = 80
ATT_LOOKAHEAD = 2
ATT_VALUE_LAG = 2


def _vmem_limit(nbytes):
    return int(min(nbytes, VMEM_BYTES - 4 * 1024 * 1024))


def _rms(x, g):
    ms = jnp.mean(x * x, axis=-1, keepdims=True)
    return x * lax.rsqrt(ms + EPS) * g


def _gelu_tanh(x):
    c = np.sqrt(2.0 / np.pi).astype(np.float32)
    inner = x * (c + (0.044715 * c) * (x * x))
    hx = 0.5 * x
    return hx + hx * jnp.tanh(inner)


def _dot(a, b):
    return jnp.dot(a, b, preferred_element_type=F32)


def _mod_kernel(c_ref, w_ref, b_ref, o_ref):
    c = c_ref[...]
    s = c * jax.nn.sigmoid(c)
    o_ref[0] = _dot(s.astype(BF16), w_ref[0].astype(BF16)) + b_ref[0]


def _modulation(cond, w_mod, b_mod):
    depth, _, width = w_mod.shape
    tn = 1536
    return pl.pallas_call(
        _mod_kernel,
        grid=(depth, width // tn),
        in_specs=[
            pl.BlockSpec((MOD_ROWS, D_MODEL), lambda l, j: (0, 0)),
            pl.BlockSpec((1, D_MODEL, tn), lambda l, j: (l, 0, j)),
            pl.BlockSpec((1, 1, tn), lambda l, j: (l, 0, j)),
        ],
        out_specs=pl.BlockSpec((1, MOD_ROWS, tn), lambda l, j: (l, 0, j)),
        out_shape=jax.ShapeDtypeStruct((depth, MOD_ROWS, width), F32),
        compiler_params=pltpu.CompilerParams(vmem_limit_bytes=_vmem_limit(40 << 20)),
        name="modulation",
    )(cond, w_mod, b_mod.reshape(depth, 1, width))


def _inproj_kernel(x_ref, mod_ref, g1_ref, w_ref, lng_ref, lnb_ref, ws_ref, bs_ref, qg_ref, kg_ref,
                   ones_ref, cq_ref, sq_ref, ck_ref, sk_ref,
                   xa_ref, yb_ref, q_ref, kd_ref, vt_ref, *, tm, mod_base, tiles_per_row):
    i = pl.program_id(0)
    row = mod_base + i // tiles_per_row
    shift = mod_ref[pl.ds(row, 1), 0:D_MODEL]
    scale = mod_ref[pl.ds(row, 1), D_MODEL:2 * D_MODEL]
    h = (_rms(x_ref[...], g1_ref[...] * (1.0 + scale)) + shift).astype(BF16)

    v_raw = _dot(h, w_ref[:, OFF_B + D_MODEL:OFF_Q])
    u_raw = _dot(h, w_ref[:, OFF_B:OFF_B + D_MODEL])
    q_raw = _dot(h, w_ref[:, OFF_Q:OFF_K])

    v = _gelu_tanh(v_raw)
    u = _gelu_tanh(u_raw)
    mu = jnp.mean(v, axis=-1, keepdims=True)
    vc = v - mu
    var = jnp.mean(vc * vc, axis=-1, keepdims=True)
    vn = (vc * lax.rsqrt(var + EPS) * lng_ref[...] + lnb_ref[...]).astype(BF16)
    n_chunks = tm // SGU_CHUNK
    for g in range(SGU_GROUPS):
        cols = slice(g * LANES, (g + 1) * LANES)
        rhs = jnp.concatenate([vn[c * SGU_CHUNK:(c + 1) * SGU_CHUNK, cols] for c in range(n_chunks)], axis=1)
        mixed = _dot(ws_ref[g], rhs)
        for c in range(n_chunks):
            rows = slice(c * SGU_CHUNK, (c + 1) * SGU_CHUNK)
            yb_ref[rows, cols] = (u[rows, cols] * (mixed[:, c * LANES:(c + 1) * LANES] + bs_ref[g])).astype(BF16)

    k_raw = _dot(h, w_ref[:, OFF_K:OFF_V])
    vv = _dot(h, w_ref[:, OFF_V:OFF_G])

    lane = lax.broadcasted_iota(jnp.int32, (tm, LANES), 1)
    first_half = (lane % (2 * ROPE_FREQS)) < ROPE_FREQS
    low_head = lane < HEAD_DIM

    def head_norm_rope(z, g_ref, c_ref, s_ref, out_ref):
        zz = (z * z).astype(BF16)
        for blk in range(z.shape[1] // MXU_DIM):
            cols = slice(blk * MXU_DIM, (blk + 1) * MXU_DIM)
            ms = _dot(zz[:, cols], ones_ref[...])
            zn = z[:, cols] * lax.rsqrt(ms + EPS)
            for s in range(MXU_DIM // LANES):
                t = zn[:, s * LANES:(s + 1) * LANES] * g_ref[...]
                sw = jnp.where(first_half, pltpu.roll(t, LANES - ROPE_FREQS, 1),
                               pltpu.roll(t, ROPE_FREQS, 1))
                slab = blk * (MXU_DIM // LANES) + s
                out_ref(slab, t * c_ref[...] + sw * s_ref[...])

    def store_q(slab, val):
        q_ref[:, slab * LANES:(slab + 1) * LANES] = val.astype(BF16)

    def dup_heads(slab_val):
        r = pltpu.roll(slab_val, HEAD_DIM, 1)
        return jnp.where(low_head, slab_val, r), jnp.where(low_head, r, slab_val)

    def store_kd(slab, val):
        a, b = dup_heads(val)
        kd_ref[:, (2 * slab) * LANES:(2 * slab + 1) * LANES] = a.astype(BF16)
        kd_ref[:, (2 * slab + 1) * LANES:(2 * slab + 2) * LANES] = b.astype(BF16)

    head_norm_rope(q_raw, qg_ref, cq_ref, sq_ref, store_q)
    head_norm_rope(k_raw, kg_ref, ck_ref, sk_ref, store_kd)

    vt = vv.T.astype(BF16)
    ones_rows = jnp.ones((VT_ROWS - HEAD_DIM, tm), BF16)
    for kh in range(N_KV_HEADS):
        vt_ref[kh * VT_ROWS:kh * VT_ROWS + HEAD_DIM, :] = vt[kh * HEAD_DIM:(kh + 1) * HEAD_DIM, :]
        vt_ref[kh * VT_ROWS + HEAD_DIM:(kh + 1) * VT_ROWS, :] = ones_rows

    xa_ref[...] = _dot(h, w_ref[:, 0:OFF_B])


def _inproj(layer, x2d, mod, g1, w_a, lng, lnb, ws, bs, qg, kg, ones_bd, tabs, *, seq, mod_base, per_batch, tm):
    n_tok = x2d.shape[0]
    tiles_per_seq = seq // tm
    tiles_per_row = tiles_per_seq if per_batch else n_tok // tm
    cq, sq, ck, sk = tabs
    const = lambda i: (0, 0)
    lay2 = lambda i: (layer, 0, 0)
    tab_map = lambda i: (i % tiles_per_seq, 0)
    kvw = 2 * N_KV_HEADS * HEAD_DIM
    return pl.pallas_call(
        functools.partial(_inproj_kernel, tm=tm, mod_base=mod_base, tiles_per_row=tiles_per_row),
        grid=(n_tok // tm,),
        in_specs=[
            pl.BlockSpec((tm, D_MODEL), lambda i: (i, 0)),
            pl.BlockSpec((None, MOD_ROWS, 6 * D_MODEL), lay2),
            pl.BlockSpec((None, 1, D_MODEL), lay2),
            pl.BlockSpec((None, D_MODEL, OFF_G), lay2, pipeline_mode=pl.Buffered(1)),
            pl.BlockSpec((None, 1, D_MODEL), lay2),
            pl.BlockSpec((None, 1, D_MODEL), lay2),
            pl.BlockSpec((None, SGU_GROUPS, SGU_CHUNK, SGU_CHUNK), lambda i: (layer, 0, 0, 0)),
            pl.BlockSpec((None, SGU_GROUPS, SGU_CHUNK, LANES), lambda i: (layer, 0, 0, 0)),
            pl.BlockSpec((None, 1, LANES), lay2),
            pl.BlockSpec((None, 1, LANES), lay2),
            pl.BlockSpec((MXU_DIM, MXU_DIM), const),
            pl.BlockSpec((tm, LANES), tab_map),
            pl.BlockSpec((tm, LANES), tab_map),
            pl.BlockSpec((tm, LANES), tab_map),
            pl.BlockSpec((tm, LANES), tab_map),
        ],
        out_specs=[
            pl.BlockSpec((tm, D_MODEL), lambda i: (i, 0)),
            pl.BlockSpec((tm, D_MODEL), lambda i: (i, 0)),
            pl.BlockSpec((tm, D_MODEL), lambda i: (i, 0)),
            pl.BlockSpec((tm, kvw), lambda i: (i, 0)),
            pl.BlockSpec((N_KV_HEADS * VT_ROWS, tm), lambda i: (0, i)),
        ],
        out_shape=[
            jax.ShapeDtypeStruct((n_tok, D_MODEL), F32),
            jax.ShapeDtypeStruct((n_tok, D_MODEL), BF16),
            jax.ShapeDtypeStruct((n_tok, D_MODEL), BF16),
            jax.ShapeDtypeStruct((n_tok, kvw), BF16),
            jax.ShapeDtypeStruct((N_KV_HEADS * VT_ROWS, n_tok), BF16),
        ],
        compiler_params=pltpu.CompilerParams(vmem_limit_bytes=_vmem_limit(56 << 20)),
        name="inproj",
    )(x2d, mod, g1, w_a, lng, lnb, ws, bs, qg, kg, ones_bd, cq, sq, ck, sk)


def _scan_kernel(*refs, tc, nchunk, nb, reverse, conv):
    if conv:
        (xa_ref, xp_ref, xn_ref, cw_ref, cb_ref, wa_ref, wx_ref, ba_ref, bx_ref, lam_ref, h0_ref,
         out_ref, hfin_ref, xc_ref, xe_ref, a_ref, u_ref, h_ref, hc_ref) = refs
    else:
        (xc_ref, hf_ref, wa_ref, wx_ref, ba_ref, bx_ref, lam_ref, h0_ref,
         out_ref, hfin_ref, a_ref, u_ref, h_ref, hc_ref) = refs
    i = pl.program_id(0)
    c = (nchunk - 1 - i) if reverse else i

    @pl.when(i == 0)
    def _():
        hc_ref[...] = h0_ref[...]

    z = -lam_ref[...]
    softplus = jnp.maximum(z, 0.0) + jnp.log1p(jnp.exp(-jnp.abs(z)))
    nc2l = (-0.5 * LRU_C * LOG2E) * softplus

    for b in range(nb):
        if conv:
            xe_ref[0:SUBLANES, :] = xp_ref[b] * (c > 0).astype(F32)
            xe_ref[SUBLANES:SUBLANES + tc, :] = xa_ref[b]
            xe_ref[SUBLANES + tc:2 * SUBLANES + tc, :] = xn_ref[b] * (c < nchunk - 1).astype(F32)
            xe = xe_ref[...]
            n_rows = tc + 2 * SUBLANES
            xc = cb_ref[...]
            for k in range(CONV_W):
                back = CONV_PAD_L - k
                xs = xe if back == 0 else pltpu.roll(xe, back % n_rows, 0)
                xc = xc + cw_ref[k:k + 1, :] * xs[SUBLANES:SUBLANES + tc]
            xc_ref[b] = xc
        else:
            xc = xc_ref[b]
        for blk in range(D_MODEL // MXU_DIM):
            cols = slice(blk * MXU_DIM, (blk + 1) * MXU_DIM)
            xcb = xc[:, cols]
            xcb16 = xcb.astype(BF16)
            t_r = jnp.tanh(_dot(xcb16, wa_ref[blk]) + ba_ref[:, cols])
            t_i = jnp.tanh(_dot(xcb16, wx_ref[blk]) + bx_ref[:, cols])
            log2_a = nc2l[:, cols] * t_r + nc2l[:, cols]
            a = jnp.exp2(log2_a)
            s = jnp.tanh(log2_a * (-LN2)) * (a * a + 1.0)
            root = jnp.where(s > 0.0, s * lax.rsqrt(s), 0.0)
            u = xcb * (t_i + 1.0) * root
            for sl in range(MXU_DIM // LANES):
                slab = blk * (MXU_DIM // LANES) + sl
                a_ref[b, pl.ds(slab, tc, stride=N_SLABS), :] = a[:, sl * LANES:(sl + 1) * LANES]
                u_ref[b, pl.ds(slab, tc, stride=N_SLABS), :] = u[:, sl * LANES:(sl + 1) * LANES]

    n_groups = tc // SCAN_GROUP

    def group(g, hs):
        gg = (n_groups - 1 - g) if reverse else g
        base = pl.multiple_of(gg * (SCAN_GROUP * N_SLABS), SCAN_GROUP * N_SLABS)
        hs = list(hs)
        for j in range(SCAN_GROUP):
            rows = pl.ds(base + ((SCAN_GROUP - 1 - j) if reverse else j) * N_SLABS, N_SLABS)
            for b in range(nb):
                hs[b] = a_ref[b, rows, :] * hs[b] + u_ref[b, rows, :]
                h_ref[b, rows, :] = hs[b]
        return tuple(hs)

    hs = lax.fori_loop(0, n_groups, group, tuple(hc_ref[b] for b in range(nb)))
    for b in range(nb):
        hc_ref[b] = hs[b]
        hfin_ref[b] = hs[b]
        for slab in range(N_SLABS):
            cols = slice(slab * LANES, (slab + 1) * LANES)
            h_tok = h_ref[b, pl.ds(slab, tc, stride=N_SLABS), :]
            if not conv:
                h_tok = h_tok + hf_ref[b, :, cols].astype(F32)
            out_ref[b, :, cols] = h_tok.astype(BF16)


def _scan(layer, direction, x_in, hf, cw, cb, wa_bd, wx_bd, ba, bx, lam, h0):
    nb, seq, _ = x_in.shape
    tc = 256
    nchunk = seq // tc
    conv = direction == 0
    reverse = direction == 1
    chunk = (lambda i: nchunk - 1 - i) if reverse else (lambda i: i)
    nblk = D_MODEL // MXU_DIM
    tok_spec = pl.BlockSpec((nb, tc, D_MODEL), lambda i: (0, chunk(i), 0))
    gate_w = pl.BlockSpec((None, None, nblk, MXU_DIM, MXU_DIM), lambda i: (layer, direction, 0, 0, 0))
    vec = pl.BlockSpec((None, None, 1, D_MODEL), lambda i: (layer, direction, 0, 0))
    state = pl.BlockSpec((None, nb, N_SLABS, LANES), lambda i: (direction, 0, 0, 0))
    in_specs = [tok_spec]
    args = [x_in]
    if conv:
        halo_blocks = seq // SUBLANES
        per = tc // SUBLANES
        in_specs += [
            pl.BlockSpec((nb, SUBLANES, D_MODEL), lambda i: (0, jnp.maximum(i * per - 1, 0), 0)),
            pl.BlockSpec((nb, SUBLANES, D_MODEL), lambda i: (0, jnp.minimum((i + 1) * per, halo_blocks - 1), 0)),
            pl.BlockSpec((None, CONV_W, D_MODEL), lambda i: (layer, 0, 0)),
            pl.BlockSpec((None, 1, D_MODEL), lambda i: (layer, 0, 0)),
        ]
        args += [x_in, x_in, cw, cb]
    else:
        in_specs.append(tok_spec)
        args.append(hf)
    in_specs += [gate_w, gate_w, vec, vec, vec, state]
    args += [wa_bd, wx_bd, ba, bx, lam, h0]
    out_specs = [
        pl.BlockSpec((nb, tc, D_MODEL), lambda i: (0, chunk(i), 0)),
        pl.BlockSpec((nb, N_SLABS, LANES), lambda i: (0, 0, 0)),
    ]
    out_shape = [
        jax.ShapeDtypeStruct((nb, seq, D_MODEL), BF16),
        jax.ShapeDtypeStruct((nb, N_SLABS, LANES), F32),
    ]
    scratch = []
    if conv:
        out_specs.append(pl.BlockSpec((nb, tc, D_MODEL), lambda i: (0, i, 0)))
        out_shape.append(jax.ShapeDtypeStruct((nb, seq, D_MODEL), F32))
        scratch.append(pltpu.VMEM((tc + 2 * SUBLANES, D_MODEL), F32))
    scratch += [pltpu.VMEM((nb, tc * N_SLABS, LANES), F32)] * 3 + [pltpu.VMEM((nb, N_SLABS, LANES), F32)]
    return pl.pallas_call(
        functools.partial(_scan_kernel, tc=tc, nchunk=nchunk, nb=nb, reverse=reverse, conv=conv),
        grid=(nchunk,),
        in_specs=in_specs,
        out_specs=out_specs,
        out_shape=out_shape,
        scratch_shapes=scratch,
        compiler_params=pltpu.CompilerParams(vmem_limit_bytes=_vmem_limit(56 << 20)),
        name="scan_fwd" if conv else "scan_bwd",
    )(*args)


def _attn_kernel(sink_ref, q_ref, *refs, layer, n_steps, qb, band):
    if band:
        kp_ref, kcur_ref, kn_ref, vp_ref, vcur_ref, vn_ref, kx_ref, vx_ref, o_ref = refs
    else:
        kx_ref, vx_ref, o_ref = refs
    n = pl.program_id(1)
    lane = lax.broadcasted_iota(jnp.int32, (1, LANES), 1)
    low = lane < HEAD_DIM
    high = jnp.logical_not(low)

    def block(j):
        return slice(j * ATT_BLOCK, (j + 1) * ATT_BLOCK)

    if band:
        kj = lax.broadcasted_iota(jnp.int32, (ATT_BLOCK, ATT_BLOCK), 0)
        qi = lax.broadcasted_iota(jnp.int32, (ATT_BLOCK, ATT_BLOCK), 1)

        def tiled(visible):
            one = jnp.where(visible, 0.0, NEG_BIG).astype(F32)
            return jnp.concatenate([one, one], axis=1)

        bias_prev = [tiled((kj >= qi) & (n > 0)) if j == 0 else tiled(kj >= qi) for j in range(qb)]
        bias_next = [tiled((kj <= qi) & (n < n_steps - 1)) if j == qb - 1 else tiled(kj <= qi) for j in range(qb)]

    def split(chain):
        j, rest = divmod(chain, 2 * N_KV_HEADS)
        kh, half = divmod(rest, 2)
        return j, kh, half

    def scores(chain):
        j, kh, half = split(chain)
        q2 = jnp.concatenate([q_ref[block(j), (2 * kh) * LANES:(2 * kh + 1) * LANES],
                              q_ref[block(j), (2 * kh + 1) * LANES:(2 * kh + 2) * LANES]], axis=0)
        q2 = jnp.where(low if half == 0 else high, q2, jnp.zeros_like(q2))
        slab = slice(kh * LANES, (kh + 1) * LANES)
        if band:
            blocks = [kp_ref[:, slab]] + [kcur_ref[block(i), slab] for i in range(qb)] + [kn_ref[:, slab]]
            kall = jnp.concatenate(blocks[j:j + 3] + [kx_ref[:, slab]], axis=0)
        else:
            kall = kx_ref[:, slab]
        return lax.dot_general(kall, q2, (((1,), (1,)), ((), ())), preferred_element_type=F32)

    def weights(chain, logits):
        j, kh, half = split(chain)
        pieces = [logits[i * ATT_BLOCK:(i + 1) * ATT_BLOCK, :] for i in range(logits.shape[0] // ATT_BLOCK)]
        if band:
            pieces[0] = pieces[0] + bias_prev[j]
            pieces[2] = pieces[2] + bias_next[j]
        h_top = 4 * kh + half
        sk = jnp.concatenate([jnp.full((1, LANES), sink_ref[layer, h_top] * LOG2E, F32),
                              jnp.full((1, LANES), sink_ref[layer, h_top + 2] * LOG2E, F32)], axis=1)
        widest = pieces[0]
        for piece in pieces[1:]:
            widest = jnp.maximum(widest, piece)
        m = jnp.maximum(sk, jnp.max(widest, axis=0, keepdims=True))
        p = jnp.concatenate([jnp.exp2(piece - m).astype(BF16) for piece in pieces], axis=0)
        return p, jnp.exp2(sk - m)

    def values(chain, p):
        j, kh, half = split(chain)
        rows = slice(kh * VT_ROWS, (kh + 1) * VT_ROWS)
        if band:
            blocks = [vp_ref[rows, :]] + [vcur_ref[rows, block(i)] for i in range(qb)] + [vn_ref[rows, :]]
            vall = jnp.concatenate(blocks[j:j + 3] + [vx_ref[rows, :]], axis=1)
        else:
            vall = vx_ref[rows, :]
        return _dot(vall, p)

    def finish(j, kh, pv, sink_p):
        res = [pv[h][:HEAD_DIM, :] / (pv[h][HEAD_DIM:HEAD_DIM + 1, :] + sink_p[h]) for h in range(2)]
        for i in range(2):
            cols = slice(i * ATT_BLOCK, (i + 1) * ATT_BLOCK)
            both = jnp.concatenate([res[0][:, cols], res[1][:, cols]], axis=0)
            o_ref[block(j), (2 * kh + i) * LANES:(2 * kh + i + 1) * LANES] = both.T.astype(BF16)

    n_chains = qb * 2 * N_KV_HEADS
    logits = {c: scores(c) for c in range(min(ATT_LOOKAHEAD, n_chains))}
    probs, sink_p, pv = {}, {}, {}
    for step in range(n_chains + ATT_VALUE_LAG):
        if step + ATT_LOOKAHEAD < n_chains:
            logits[step + ATT_LOOKAHEAD] = scores(step + ATT_LOOKAHEAD)
        if step < n_chains:
            probs[step], sink_p[step] = weights(step, logits.pop(step))
        c = step - ATT_VALUE_LAG
        if c >= 0:
            pv[c] = values(c, probs.pop(c))
            if c % 2 == 1:
                j, kh, _ = split(c)
                finish(j, kh, [pv.pop(c - 1), pv.pop(c)], [sink_p.pop(c - 1), sink_p.pop(c)])


def _attention(layer, sink, q, kd, vt, kdx, vtx, *, band, qb=ATT_QB):
    nb, seq, _ = q.shape
    nctx = kdx.shape[1]
    nblk = seq // ATT_BLOCK
    qb = min(qb, nblk)
    assert nblk % qb == 0
    n_steps = nblk // qb
    kvw = kdx.shape[-1]
    vtw = vtx.shape[0]
    qspec = pl.BlockSpec((None, qb * ATT_BLOCK, D_MODEL), lambda b, n: (b, n, 0))
    in_specs = [pl.BlockSpec(memory_space=pltpu.SMEM), qspec]
    args = [sink, q]
    if band:
        first = lambda b, n: jnp.maximum(n * qb - 1, 0)
        last = lambda b, n: jnp.minimum((n + 1) * qb, nblk - 1)
        in_specs += [
            pl.BlockSpec((None, ATT_BLOCK, kvw), lambda b, n: (b, first(b, n), 0)),
            pl.BlockSpec((None, qb * ATT_BLOCK, kvw), lambda b, n: (b, n, 0)),
            pl.BlockSpec((None, ATT_BLOCK, kvw), lambda b, n: (b, last(b, n), 0)),
            pl.BlockSpec((vtw, ATT_BLOCK), lambda b, n: (0, b * nblk + first(b, n))),
            pl.BlockSpec((vtw, qb * ATT_BLOCK), lambda b, n: (0, b * n_steps + n)),
            pl.BlockSpec((vtw, ATT_BLOCK), lambda b, n: (0, b * nblk + last(b, n))),
        ]
        args += [kd, kd, kd, vt, vt, vt]
    in_specs += [pl.BlockSpec((None, nctx, kvw), lambda b, n: (b, 0, 0)),
                 pl.BlockSpec((vtw, nctx), lambda b, n: (0, b))]
    args += [kdx, vtx]
    return pl.pallas_call(
        functools.partial(_attn_kernel, layer=layer, n_steps=n_steps, qb=qb, band=band),
        grid=(nb, n_steps),
        in_specs=in_specs,
        out_specs=pl.BlockSpec((None, qb * ATT_BLOCK, D_MODEL), lambda b, n: (b, n, 0)),
        out_shape=jax.ShapeDtypeStruct((nb, seq, D_MODEL), BF16),
        compiler_params=pltpu.CompilerParams(vmem_limit_bytes=_vmem_limit(40 << 20)),
        name="attention_band" if band else "attention_ctx",
    )(*args)


def _merge_mlp_kernel(x_ref, mod_ref, g1_ref, g2_ref, wga_ref, wgb_ref, ya_ref, yb_ref, yc_ref, wb_ref, wo_ref,
                      w1_ref, w2_ref, o_ref, *, mod_base, tiles_per_row):
    i = pl.program_id(0)
    row = mod_base + i // tiles_per_row

    def mod(k):
        return mod_ref[pl.ds(row, 1), k * D_MODEL:(k + 1) * D_MODEL]

    def gate_logits(h1, k):
        parts = []
        for ref, base in ((wga_ref, 0), (wgb_ref, GATE_BLK)):
            lo = max(k * D_MODEL, base) - base
            hi = min((k + 1) * D_MODEL, base + GATE_BLK) - base
            if hi > lo:
                parts.append(_dot(h1, ref[:, lo:hi]))
        return parts[0] if len(parts) == 1 else jnp.concatenate(parts, axis=1)

    subs = [slice(j * MERGE_SUB, (j + 1) * MERGE_SUB) for j in range(x_ref.shape[0] // MERGE_SUB)]
    xs = [x_ref[r, :] for r in subs]
    ms = []
    for j, r in enumerate(subs):
        h1 = (_rms(xs[j], g1_ref[...] * (1.0 + mod(1))) + mod(0)).astype(BF16)
        branches = (ya_ref[r, :], yb_ref[r, :], yc_ref[r, :])
        m = None
        for k in range(3):
            gate = jax.nn.sigmoid(gate_logits(h1, k))
            term = gate * _dot(branches[k], wb_ref[k])
            m = term if m is None else m + term
        ms.append(m.astype(BF16))
    x1s = [xs[j] + mod(2) * _dot(ms[j], wo_ref[...]) for j in range(len(subs))]
    h2s = [(_rms(x1, g2_ref[...] * (1.0 + mod(4))) + mod(3)).astype(BF16) for x1 in x1s]
    acc = list(x1s)
    ff = D_FF // FF_CHUNKS
    for c in range(FF_CHUNKS):
        fs = [jnp.maximum(_dot(h2, w1_ref[:, c * ff:(c + 1) * ff]), 0.0) for h2 in h2s]
        for j in range(len(subs)):
            acc[j] = acc[j] + mod(5) * _dot((fs[j] * fs[j]).astype(BF16), w2_ref[c * ff:(c + 1) * ff, :])
    for j, r in enumerate(subs):
        o_ref[r, :] = acc[j]


def _merge_mlp(layer, x2d, mod, g1, g2, wg, ya, yb, yc, wb, wo, w1, w2, *, mod_base, tiles_per_row, tm):
    n_tok = x2d.shape[0]
    const = lambda i: (0, 0)
    lay2 = lambda i: (layer, 0, 0)
    once = pl.Buffered(1)
    tok = lambda i: (i, 0)
    return pl.pallas_call(
        functools.partial(_merge_mlp_kernel, mod_base=mod_base, tiles_per_row=tiles_per_row),
        grid=(n_tok // tm,),
        in_specs=[
            pl.BlockSpec((tm, D_MODEL), tok),
            pl.BlockSpec((None, MOD_ROWS, 6 * D_MODEL), lay2),
            pl.BlockSpec((None, 1, D_MODEL), lay2),
            pl.BlockSpec((None, 1, D_MODEL), lay2),
            pl.BlockSpec((None, D_MODEL, GATE_BLK), lambda i: (layer, 0, OFF_G // GATE_BLK), pipeline_mode=once),
            pl.BlockSpec((None, D_MODEL, GATE_BLK), lambda i: (layer, 0, OFF_G // GATE_BLK + 1), pipeline_mode=once),
            pl.BlockSpec((tm, D_MODEL), tok),
            pl.BlockSpec((tm, D_MODEL), tok),
            pl.BlockSpec((tm, D_MODEL), tok),
            pl.BlockSpec((None, 3, D_MODEL, D_MODEL), lambda i: (layer, 0, 0, 0), pipeline_mode=once),
            pl.BlockSpec((None, D_MODEL, D_MODEL), lay2, pipeline_mode=once),
            pl.BlockSpec((None, D_MODEL, D_FF), lay2, pipeline_mode=once),
            pl.BlockSpec((None, D_FF, D_MODEL), lay2, pipeline_mode=once),
        ],
        out_specs=pl.BlockSpec((tm, D_MODEL), tok),
        out_shape=jax.ShapeDtypeStruct((n_tok, D_MODEL), F32),
        compiler_params=pltpu.CompilerParams(vmem_limit_bytes=_vmem_limit(60 << 20)),
        name="merge_mlp",
    )(x2d, mod, g1, g2, wg, wg, ya, yb, yc, wb, wo, w1, w2)


def _rope_tables(seq):
    pos = jnp.arange(seq)
    row = (pos // GRID_W).astype(F32)
    col = (pos % GRID_W).astype(F32)
    inv = jnp.power(ROPE_BASE, -jnp.arange(ROPE_FREQS, dtype=F32) / ROPE_FREQS)
    ang_r = row[:, None] * inv
    ang_c = col[:, None] * inv
    cos = jnp.concatenate([jnp.cos(ang_r), jnp.cos(ang_r), jnp.cos(ang_c), jnp.cos(ang_c)], axis=-1)
    sin = jnp.concatenate([-jnp.sin(ang_r), jnp.sin(ang_r), -jnp.sin(ang_c), jnp.sin(ang_c)], axis=-1)
    reps = LANES // HEAD_DIM
    return jnp.tile(cos, (1, reps)), jnp.tile(sin, (1, reps))


def _block_diag(w):
    depth = w.shape[0]
    per = MXU_DIM // RNN_BLOCK
    rows = w.reshape(depth, 2, RNN_BLOCKS // per, MXU_DIM, RNN_BLOCK)
    tiled = jnp.tile(rows, (1, 1, 1, 1, per))
    blk = np.arange(MXU_DIM) // RNN_BLOCK
    return jnp.where(blk[:, None] == blk[None, :], tiled, 0.0).astype(BF16)


def kernel(x, c, ctx, c_ctx, w_mod, b_mod, g_norm1, w_in, conv_w, conv_b, lru_wa, lru_ba, lru_wx, lru_bx,
           lru_lambda, sgu_ln_g, sgu_ln_b, sgu_w, sgu_b, q_norm_g, k_norm_g, sink, w_branch, w_out, g_norm2,
           w_ff1, w_ff2):
    n_batch, n_tok, _ = x.shape
    n_ctx = ctx.shape[1]
    depth = w_mod.shape[0]
    tm_x, tm_c = 512, 512
    assert n_batch + 1 <= MOD_ROWS and n_tok % tm_x == 0 and (n_batch * n_ctx) % tm_c == 0

    cond = jnp.zeros((MOD_ROWS, D_MODEL), F32).at[:n_batch].set(c).at[n_batch].set(c_ctx)
    mod = _modulation(cond, w_mod, b_mod)

    cos, sin = _rope_tables(n_tok)
    q_mul = ATT_SCALE * LOG2E
    tabs_x = (cos * q_mul, sin * q_mul, cos, sin)
    one = jnp.ones((n_batch * n_ctx, LANES), F32)
    zero = jnp.zeros((n_batch * n_ctx, LANES), F32)
    tabs_c = (one * q_mul, zero, one, zero)

    head = np.arange(MXU_DIM) // HEAD_DIM
    ones_bd = jnp.asarray((head[:, None] == head[None, :]) * (1.0 / HEAD_DIM), BF16)

    w_in16 = w_in.astype(BF16)
    wb16 = w_branch.astype(BF16)
    wo16 = w_out.astype(BF16)
    w116 = w_ff1.astype(BF16)
    w216 = w_ff2.astype(BF16)
    ws16 = sgu_w.astype(BF16)
    wa_bd = _block_diag(lru_wa)
    wx_bd = _block_diag(lru_wx)
    ba_h = (0.5 * lru_ba)[:, :, None]
    bx_h = (0.5 * lru_bx)[:, :, None]
    lam = lru_lambda[:, :, None]
    conv_wh = 0.5 * conv_w
    conv_bh = (0.5 * conv_b)[:, None]

    x2d = x.reshape(n_batch * n_tok, D_MODEL)
    cx2d = ctx.reshape(n_batch * n_ctx, D_MODEL)
    h0 = jnp.zeros((2, n_batch, N_SLABS, LANES), F32)
    reps = LANES // HEAD_DIM

    def rglru(layer, xa, init):
        scan_p = (wa_bd, wx_bd, ba_h, bx_h, lam)
        hf, hf_fin, xc = _scan(layer, 0, xa, None, conv_wh, conv_bh, *scan_p, init)
        ya, hb_fin = _scan(layer, 1, xc, hf, None, None, *scan_p, init)
        return ya, jnp.stack([hf_fin, hb_fin])

    g1 = g_norm1[:, None]
    g2 = g_norm2[:, None]
    lng = sgu_ln_g[:, None]
    lnb = sgu_ln_b[:, None]
    bs = jnp.broadcast_to(sgu_b[:, :, :, None], (depth, SGU_GROUPS, SGU_CHUNK, LANES))
    qg = jnp.tile(q_norm_g, (1, reps))[:, None]
    kg = jnp.tile(k_norm_g, (1, reps))[:, None]

    for l in range(depth):
        last = l == depth - 1
        inproj_p = (mod, g1, w_in16, lng, lnb, ws16, bs, qg, kg, ones_bd)
        merge_w = (wb16, wo16, w116, w216)

        xa_c, yb_c, q_c, kd_c, vt_c = _inproj(l, cx2d, *inproj_p, tabs_c, seq=n_batch * n_ctx, mod_base=n_batch,
                                              per_batch=False, tm=tm_c)
        ya_c, hfin_c = rglru(l, xa_c.reshape(n_batch, n_ctx, D_MODEL), h0)
        kd_c = kd_c.reshape(n_batch, n_ctx, -1)

        xa, yb, q, kd, vt = _inproj(l, x2d, *inproj_p, tabs_x, seq=n_tok, mod_base=0, per_batch=True, tm=tm_x)
        ya_x, _ = rglru(l, xa.reshape(n_batch, n_tok, D_MODEL), hfin_c)
        yc = _attention(l, sink, q.reshape(n_batch, n_tok, D_MODEL), kd.reshape(n_batch, n_tok, -1), vt,
                        kd_c, vt_c, band=True)
        x2d = _merge_mlp(l, x2d, mod, g1, g2, w_in16, ya_x.reshape(n_batch * n_tok, D_MODEL), yb,
                         yc.reshape(n_batch * n_tok, D_MODEL), *merge_w,
                         mod_base=0, tiles_per_row=n_tok // tm_x, tm=tm_x)

        if not last:
            yc_c = _attention(l, sink, q_c.reshape(n_batch, n_ctx, D_MODEL), None, None, kd_c, vt_c, band=False)
            cx2d = _merge_mlp(l, cx2d, mod, g1, g2, w_in16, ya_c.reshape(n_batch * n_ctx, D_MODEL), yb_c,
                              yc_c.reshape(n_batch * n_ctx, D_MODEL), *merge_w,
                              mod_base=n_batch, tiles_per_row=n_batch * n_ctx // tm_c, tm=tm_c)

    return x2d.reshape(n_batch, n_tok, D_MODEL)
```

```python
import functools

import numpy as np
import jax
import jax.numpy as jnp
from jax import lax
from jax.experimental import pallas as pl
from jax.experimental.pallas import tpu as pltpu

F32 = jnp.float32
BF16 = jnp.bfloat16

D_MODEL = 1024
GRID_W = 64
EPS = 1e-6
RNN_BLOCKS = 16
RNN_BLOCK = D_MODEL // RNN_BLOCKS
CONV_W = 4
CONV_PAD_L = 2
LRU_C = 8.0
SGU_CHUNK = 128
SGU_GROUPS = 8
N_HEADS = 16
N_KV_HEADS = 4
HEAD_DIM = 64
WINDOW = 128
ATT_BLOCK = 128
ATT_SCALE = HEAD_DIM ** -0.5
ROPE_BASE = 10000.0
ROPE_FREQS = HEAD_DIM // 4
D_FF = 4 * D_MODEL
OFF_B = D_MODEL
OFF_Q = OFF_B + 2 * D_MODEL
OFF_K = OFF_Q + N_HEADS * HEAD_DIM
OFF_V = OFF_K + N_KV_HEADS * HEAD_DIM
OFF_G = OFF_V + N_KV_HEADS * HEAD_DIM

LANES = 128
SUBLANES = 8
MXU_DIM = 256
VMEM_BYTES = 64 * 1024 * 1024

N_SLABS = D_MODEL // LANES
MOD_ROWS = SUBLANES
NEG_BIG = -1e30
LOG2E = 1.4426950408889634
LN2 = 0.6931471805599453
SCAN_GROUP = 16
GATE_BLK = 1536
MERGE_SUB = 256
FF_CHUNKS = 2
ATT_QB = 8
ATT_LOOKAHEAD = 2
ATT_VALUE_LAG = 2


def _vmem_limit(nbytes):
    return int(min(nbytes, VMEM_BYTES - 4 * 1024 * 1024))


def _rms(x, g):
    ms = jnp.mean(x * x, axis=-1, keepdims=True)
    return x * lax.rsqrt(ms + EPS) * g


def _gelu_tanh(x):
    c = np.sqrt(2.0 / np.pi).astype(np.float32)
    inner = x * (c + (0.044715 * c) * (x * x))
    hx = 0.5 * x
    return hx + hx * jnp.tanh(inner)


def _dot(a, b):
    return jnp.dot(a, b, preferred_element_type=F32)


def _mod_kernel(c_ref, w_ref, b_ref, o_ref):
    c = c_ref[...]
    s = c * jax.nn.sigmoid(c)
    o_ref[0] = _dot(s.astype(BF16), w_ref[0].astype(BF16)) + b_ref[0]


def _modulation(cond, w_mod, b_mod):
    depth, _, width = w_mod.shape
    tn = 1536
    return pl.pallas_call(
        _mod_kernel,
        grid=(depth, width // tn),
        in_specs=[
            pl.BlockSpec((MOD_ROWS, D_MODEL), lambda l, j: (0, 0)),
            pl.BlockSpec((1, D_MODEL, tn), lambda l, j: (l, 0, j)),
            pl.BlockSpec((1, 1, tn), lambda l, j: (l, 0, j)),
        ],
        out_specs=pl.BlockSpec((1, MOD_ROWS, tn), lambda l, j: (l, 0, j)),
        out_shape=jax.ShapeDtypeStruct((depth, MOD_ROWS, width), F32),
        compiler_params=pltpu.CompilerParams(vmem_limit_bytes=_vmem_limit(40 << 20)),
        name="modulation",
    )(cond, w_mod, b_mod.reshape(depth, 1, width))


def _inproj_kernel(x_ref, mod_ref, g1_ref, w_ref, lng_ref, lnb_ref, ws_ref, bs_ref, qg_ref, kg_ref,
                   ones_ref, cq_ref, sq_ref, ck_ref, sk_ref,
                   xa_ref, yb_ref, q_ref, kd_ref, vt_ref, *, tm, mod_base, tiles_per_row):
    i = pl.program_id(0)
    row = mod_base + i // tiles_per_row
    shift = mod_ref[pl.ds(row, 1), 0:D_MODEL]
    scale = mod_ref[pl.ds(row, 1), D_MODEL:2 * D_MODEL]
    h = (_rms(x_ref[...], g1_ref[...] * (1.0 + scale)) + shift).astype(BF16)

    v_raw = _dot(h, w_ref[:, OFF_B + D_MODEL:OFF_Q])
    u_raw = _dot(h, w_ref[:, OFF_B:OFF_B + D_MODEL])
    q_raw = _dot(h, w_ref[:, OFF_Q:OFF_K])

    v = _gelu_tanh(v_raw)
    u = _gelu_tanh(u_raw)
    mu = jnp.mean(v, axis=-1, keepdims=True)
    vc = v - mu
    var = jnp.mean(vc * vc, axis=-1, keepdims=True)
    vn = (vc * lax.rsqrt(var + EPS) * lng_ref[...] + lnb_ref[...]).astype(BF16)
    n_chunks = tm // SGU_CHUNK
    for g in range(SGU_GROUPS):
        cols = slice(g * LANES, (g + 1) * LANES)
        rhs = jnp.concatenate([vn[c * SGU_CHUNK:(c + 1) * SGU_CHUNK, cols] for c in range(n_chunks)], axis=1)
        mixed = _dot(ws_ref[g], rhs)
        for c in range(n_chunks):
            rows = slice(c * SGU_CHUNK, (c + 1) * SGU_CHUNK)
            yb_ref[rows, cols] = (u[rows, cols] * (mixed[:, c * LANES:(c + 1) * LANES] + bs_ref[g])).astype(BF16)

    k_raw = _dot(h, w_ref[:, OFF_K:OFF_V])
    vv = _dot(h, w_ref[:, OFF_V:OFF_G])

    lane = lax.broadcasted_iota(jnp.int32, (tm, LANES), 1)
    first_half = (lane % (2 * ROPE_FREQS)) < ROPE_FREQS
    low_head = lane < HEAD_DIM

    def head_norm_rope(z, g_ref, c_ref, s_ref, out_ref):
        zz = (z * z).astype(BF16)
        for blk in range(z.shape[1] // MXU_DIM):
            cols = slice(blk * MXU_DIM, (blk + 1) * MXU_DIM)
            ms = _dot(zz[:, cols], ones_ref[...])
            zn = z[:, cols] * lax.rsqrt(ms + EPS)
            for s in range(MXU_DIM // LANES):
                t = zn[:, s * LANES:(s + 1) * LANES] * g_ref[...]
                sw = jnp.where(first_half, pltpu.roll(t, LANES - ROPE_FREQS, 1),
                               pltpu.roll(t, ROPE_FREQS, 1))
                slab = blk * (MXU_DIM // LANES) + s
                out_ref(slab, t * c_ref[...] + sw * s_ref[...])

    def store_q(slab, val):
        q_ref[:, slab * LANES:(slab + 1) * LANES] = val.astype(BF16)

    def dup_heads(slab_val):
        r = pltpu.roll(slab_val, HEAD_DIM, 1)
        return jnp.where(low_head, slab_val, r), jnp.where(low_head, r, slab_val)

    def store_kd(slab, val):
        a, b = dup_heads(val)
        kd_ref[:, (2 * slab) * LANES:(2 * slab + 1) * LANES] = a.astype(BF16)
        kd_ref[:, (2 * slab + 1) * LANES:(2 * slab + 2) * LANES] = b.astype(BF16)

    head_norm_rope(q_raw, qg_ref, cq_ref, sq_ref, store_q)
    head_norm_rope(k_raw, kg_ref, ck_ref, sk_ref, store_kd)

    vt = vv.T.astype(BF16)
    ones_rows = jnp.ones((HEAD_DIM, tm), BF16)
    for kh in range(N_KV_HEADS):
        vt_ref[(2 * kh) * HEAD_DIM:(2 * kh + 1) * HEAD_DIM, :] = vt[kh * HEAD_DIM:(kh + 1) * HEAD_DIM, :]
        vt_ref[(2 * kh + 1) * HEAD_DIM:(2 * kh + 2) * HEAD_DIM, :] = ones_rows

    xa_ref[...] = _dot(h, w_ref[:, 0:OFF_B])


def _inproj(layer, x2d, mod, g1, w_a, lng, lnb, ws, bs, qg, kg, ones_bd, tabs, *, seq, mod_base, per_batch, tm):
    n_tok = x2d.shape[0]
    tiles_per_seq = seq // tm
    tiles_per_row = tiles_per_seq if per_batch else n_tok // tm
    cq, sq, ck, sk = tabs
    const = lambda i: (0, 0)
    lay2 = lambda i: (layer, 0, 0)
    tab_map = lambda i: (i % tiles_per_seq, 0)
    kvw = 2 * N_KV_HEADS * HEAD_DIM
    return pl.pallas_call(
        functools.partial(_inproj_kernel, tm=tm, mod_base=mod_base, tiles_per_row=tiles_per_row),
        grid=(n_tok // tm,),
        in_specs=[
            pl.BlockSpec((tm, D_MODEL), lambda i: (i, 0)),
            pl.BlockSpec((None, MOD_ROWS, 6 * D_MODEL), lay2),
            pl.BlockSpec((None, 1, D_MODEL), lay2),
            pl.BlockSpec((None, D_MODEL, OFF_G), lay2, pipeline_mode=pl.Buffered(1)),
            pl.BlockSpec((None, 1, D_MODEL), lay2),
            pl.BlockSpec((None, 1, D_MODEL), lay2),
            pl.BlockSpec((None, SGU_GROUPS, SGU_CHUNK, SGU_CHUNK), lambda i: (layer, 0, 0, 0)),
            pl.BlockSpec((None, SGU_GROUPS, SGU_CHUNK, LANES), lambda i: (layer, 0, 0, 0)),
            pl.BlockSpec((None, 1, LANES), lay2),
            pl.BlockSpec((None, 1, LANES), lay2),
            pl.BlockSpec((MXU_DIM, MXU_DIM), const),
            pl.BlockSpec((tm, LANES), tab_map),
            pl.BlockSpec((tm, LANES), tab_map),
            pl.BlockSpec((tm, LANES), tab_map),
            pl.BlockSpec((tm, LANES), tab_map),
        ],
        out_specs=[
            pl.BlockSpec((tm, D_MODEL), lambda i: (i, 0)),
            pl.BlockSpec((tm, D_MODEL), lambda i: (i, 0)),
            pl.BlockSpec((tm, D_MODEL), lambda i: (i, 0)),
            pl.BlockSpec((tm, kvw), lambda i: (i, 0)),
            pl.BlockSpec((kvw, tm), lambda i: (0, i)),
        ],
        out_shape=[
            jax.ShapeDtypeStruct((n_tok, D_MODEL), F32),
            jax.ShapeDtypeStruct((n_tok, D_MODEL), BF16),
            jax.ShapeDtypeStruct((n_tok, D_MODEL), BF16),
            jax.ShapeDtypeStruct((n_tok, kvw), BF16),
            jax.ShapeDtypeStruct((kvw, n_tok), BF16),
        ],
        compiler_params=pltpu.CompilerParams(vmem_limit_bytes=_vmem_limit(56 << 20)),
        name="inproj",
    )(x2d, mod, g1, w_a, lng, lnb, ws, bs, qg, kg, ones_bd, cq, sq, ck, sk)


def _scan_kernel(*refs, tc, nchunk, nb, reverse, conv):
    if conv:
        (xa_ref, xp_ref, xn_ref, cw_ref, cb_ref, wa_ref, wx_ref, ba_ref, bx_ref, lam_ref, h0_ref,
         out_ref, hfin_ref, xc_ref, xe_ref, a_ref, u_ref, h_ref, hc_ref) = refs
    else:
        (xc_ref, hf_ref, wa_ref, wx_ref, ba_ref, bx_ref, lam_ref, h0_ref,
         out_ref, hfin_ref, a_ref, u_ref, h_ref, hc_ref) = refs
    i = pl.program_id(0)
    c = (nchunk - 1 - i) if reverse else i

    @pl.when(i == 0)
    def _():
        hc_ref[...] = h0_ref[...]

    z = -lam_ref[...]
    softplus = jnp.maximum(z, 0.0) + jnp.log1p(jnp.exp(-jnp.abs(z)))
    nc2l = (-0.5 * LRU_C * LOG2E) * softplus

    for b in range(nb):
        if conv:
            xe_ref[0:SUBLANES, :] = xp_ref[b] * (c > 0).astype(F32)
            xe_ref[SUBLANES:SUBLANES + tc, :] = xa_ref[b]
            xe_ref[SUBLANES + tc:2 * SUBLANES + tc, :] = xn_ref[b] * (c < nchunk - 1).astype(F32)
            xe = xe_ref[...]
            n_rows = tc + 2 * SUBLANES
            xc = cb_ref[...]
            for k in range(CONV_W):
                back = CONV_PAD_L - k
                xs = xe if back == 0 else pltpu.roll(xe, back % n_rows, 0)
                xc = xc + cw_ref[k:k + 1, :] * xs[SUBLANES:SUBLANES + tc]
            xc_ref[b] = xc
        else:
            xc = xc_ref[b]
        for blk in range(D_MODEL // MXU_DIM):
            cols = slice(blk * MXU_DIM, (blk + 1) * MXU_DIM)
            xcb = xc[:, cols]
            xcb16 = xcb.astype(BF16)
            t_r = jnp.tanh(_dot(xcb16, wa_ref[blk]) + ba_ref[:, cols])
            t_i = jnp.tanh(_dot(xcb16, wx_ref[blk]) + bx_ref[:, cols])
            log2_a = nc2l[:, cols] * t_r + nc2l[:, cols]
            a = jnp.exp2(log2_a)
            s = jnp.tanh(log2_a * (-LN2)) * (a * a + 1.0)
            root = jnp.where(s > 0.0, s * lax.rsqrt(s), 0.0)
            u = xcb * (t_i + 1.0) * root
            for sl in range(MXU_DIM // LANES):
                slab = blk * (MXU_DIM // LANES) + sl
                a_ref[b, pl.ds(slab, tc, stride=N_SLABS), :] = a[:, sl * LANES:(sl + 1) * LANES]
                u_ref[b, pl.ds(slab, tc, stride=N_SLABS), :] = u[:, sl * LANES:(sl + 1) * LANES]

    n_groups = tc // SCAN_GROUP

    def group(g, hs):
        gg = (n_groups - 1 - g) if reverse else g
        base = pl.multiple_of(gg * (SCAN_GROUP * N_SLABS), SCAN_GROUP * N_SLABS)
        hs = list(hs)
        for j in range(SCAN_GROUP):
            rows = pl.ds(base + ((SCAN_GROUP - 1 - j) if reverse else j) * N_SLABS, N_SLABS)
            for b in range(nb):
                hs[b] = a_ref[b, rows, :] * hs[b] + u_ref[b, rows, :]
                h_ref[b, rows, :] = hs[b]
        return tuple(hs)

    hs = lax.fori_loop(0, n_groups, group, tuple(hc_ref[b] for b in range(nb)))
    for b in range(nb):
        hc_ref[b] = hs[b]
        hfin_ref[b] = hs[b]
        for slab in range(N_SLABS):
            cols = slice(slab * LANES, (slab + 1) * LANES)
            h_tok = h_ref[b, pl.ds(slab, tc, stride=N_SLABS), :]
            if not conv:
                h_tok = h_tok + hf_ref[b, :, cols].astype(F32)
            out_ref[b, :, cols] = h_tok.astype(BF16)


def _scan(layer, direction, x_in, hf, cw, cb, wa_bd, wx_bd, ba, bx, lam, h0):
    nb, seq, _ = x_in.shape
    tc = 256
    nchunk = seq // tc
    conv = direction == 0
    reverse = direction == 1
    chunk = (lambda i: nchunk - 1 - i) if reverse else (lambda i: i)
    nblk = D_MODEL // MXU_DIM
    tok_spec = pl.BlockSpec((nb, tc, D_MODEL), lambda i: (0, chunk(i), 0))
    gate_w = pl.BlockSpec((None, None, nblk, MXU_DIM, MXU_DIM), lambda i: (layer, direction, 0, 0, 0))
    vec = pl.BlockSpec((None, None, 1, D_MODEL), lambda i: (layer, direction, 0, 0))
    state = pl.BlockSpec((None, nb, N_SLABS, LANES), lambda i: (direction, 0, 0, 0))
    in_specs = [tok_spec]
    args = [x_in]
    if conv:
        halo_blocks = seq // SUBLANES
        per = tc // SUBLANES
        in_specs += [
            pl.BlockSpec((nb, SUBLANES, D_MODEL), lambda i: (0, jnp.maximum(i * per - 1, 0), 0)),
            pl.BlockSpec((nb, SUBLANES, D_MODEL), lambda i: (0, jnp.minimum((i + 1) * per, halo_blocks - 1), 0)),
            pl.BlockSpec((None, CONV_W, D_MODEL), lambda i: (layer, 0, 0)),
            pl.BlockSpec((None, 1, D_MODEL), lambda i: (layer, 0, 0)),
        ]
        args += [x_in, x_in, cw, cb]
    else:
        in_specs.append(tok_spec)
        args.append(hf)
    in_specs += [gate_w, gate_w, vec, vec, vec, state]
    args += [wa_bd, wx_bd, ba, bx, lam, h0]
    out_specs = [
        pl.BlockSpec((nb, tc, D_MODEL), lambda i: (0, chunk(i), 0)),
        pl.BlockSpec((nb, N_SLABS, LANES), lambda i: (0, 0, 0)),
    ]
    out_shape = [
        jax.ShapeDtypeStruct((nb, seq, D_MODEL), BF16),
        jax.ShapeDtypeStruct((nb, N_SLABS, LANES), F32),
    ]
    scratch = []
    if conv:
        out_specs.append(pl.BlockSpec((nb, tc, D_MODEL), lambda i: (0, i, 0)))
        out_shape.append(jax.ShapeDtypeStruct((nb, seq, D_MODEL), F32))
        scratch.append(pltpu.VMEM((tc + 2 * SUBLANES, D_MODEL), F32))
    scratch += [pltpu.VMEM((nb, tc * N_SLABS, LANES), F32)] * 3 + [pltpu.VMEM((nb, N_SLABS, LANES), F32)]
    return pl.pallas_call(
        functools.partial(_scan_kernel, tc=tc, nchunk=nchunk, nb=nb, reverse=reverse, conv=conv),
        grid=(nchunk,),
        in_specs=in_specs,
        out_specs=out_specs,
        out_shape=out_shape,
        scratch_shapes=scratch,
        compiler_params=pltpu.CompilerParams(vmem_limit_bytes=_vmem_limit(56 << 20)),
        name="scan_fwd" if conv else "scan_bwd",
    )(*args)


def _scan_both_kernel(xa_ref, cw_ref, cb_ref, wa_ref, wx_ref, ba_ref, bx_ref, lam_ref, h0_ref,
                      out_ref, hfin_ref, xe_ref, a_ref, u_ref, h_ref, hf_ref, *, tc, nb):
    d = pl.program_id(0)
    z = -lam_ref[...]
    softplus = jnp.maximum(z, 0.0) + jnp.log1p(jnp.exp(-jnp.abs(z)))
    nc2l = (-0.5 * LRU_C * LOG2E) * softplus
    n_groups = tc // SCAN_GROUP
    n_rows = tc + 2 * SUBLANES

    def run(reverse):
        for b in range(nb):
            xe_ref[0:SUBLANES, :] = jnp.zeros((SUBLANES, D_MODEL), F32)
            xe_ref[SUBLANES:SUBLANES + tc, :] = xa_ref[b]
            xe_ref[SUBLANES + tc:n_rows, :] = jnp.zeros((SUBLANES, D_MODEL), F32)
            xe = xe_ref[...]
            xc = cb_ref[...]
            for k in range(CONV_W):
                back = CONV_PAD_L - k
                xs = xe if back == 0 else pltpu.roll(xe, back % n_rows, 0)
                xc = xc + cw_ref[k:k + 1, :] * xs[SUBLANES:SUBLANES + tc]
            for blk in range(D_MODEL // MXU_DIM):
                cols = slice(blk * MXU_DIM, (blk + 1) * MXU_DIM)
                xcb = xc[:, cols]
                xcb16 = xcb.astype(BF16)
                t_r = jnp.tanh(_dot(xcb16, wa_ref[blk]) + ba_ref[:, cols])
                t_i = jnp.tanh(_dot(xcb16, wx_ref[blk]) + bx_ref[:, cols])
                log2_a = nc2l[:, cols] * t_r + nc2l[:, cols]
                a = jnp.exp2(log2_a)
                s = jnp.tanh(log2_a * (-LN2)) * (a * a + 1.0)
                root = jnp.where(s > 0.0, s * lax.rsqrt(s), 0.0)
                u = xcb * (t_i + 1.0) * root
                for sl in range(MXU_DIM // LANES):
                    slab = blk * (MXU_DIM // LANES) + sl
                    a_ref[b, pl.ds(slab, tc, stride=N_SLABS), :] = a[:, sl * LANES:(sl + 1) * LANES]
                    u_ref[b, pl.ds(slab, tc, stride=N_SLABS), :] = u[:, sl * LANES:(sl + 1) * LANES]

        def group(g, hs):
            gg = (n_groups - 1 - g) if reverse else g
            base = pl.multiple_of(gg * (SCAN_GROUP * N_SLABS), SCAN_GROUP * N_SLABS)
            hs = list(hs)
            for j in range(SCAN_GROUP):
                rows = pl.ds(base + ((SCAN_GROUP - 1 - j) if reverse else j) * N_SLABS, N_SLABS)
                for b in range(nb):
                    hs[b] = a_ref[b, rows, :] * hs[b] + u_ref[b, rows, :]
                    h_ref[b, rows, :] = hs[b]
            return tuple(hs)

        hs = lax.fori_loop(0, n_groups, group, tuple(h0_ref[b] for b in range(nb)))
        for b in range(nb):
            hfin_ref[b] = hs[b]
            for slab in range(N_SLABS):
                cols = slice(slab * LANES, (slab + 1) * LANES)
                h_tok = h_ref[b, pl.ds(slab, tc, stride=N_SLABS), :]
                if reverse:
                    out_ref[b, :, cols] = (h_tok + hf_ref[b, :, cols]).astype(BF16)
                else:
                    hf_ref[b, :, cols] = h_tok

    @pl.when(d == 0)
    def _():
        run(False)

    @pl.when(d == 1)
    def _():
        run(True)


def _scan_both(layer, xa, cw, cb, wa_bd, wx_bd, ba, bx, lam, h0):
    nb, seq, _ = xa.shape
    tc = seq
    nblk = D_MODEL // MXU_DIM
    gate_w = pl.BlockSpec((None, None, nblk, MXU_DIM, MXU_DIM), lambda d: (layer, d, 0, 0, 0))
    vec = pl.BlockSpec((None, None, 1, D_MODEL), lambda d: (layer, d, 0, 0))
    state = pl.BlockSpec((None, nb, N_SLABS, LANES), lambda d: (d, 0, 0, 0))
    return pl.pallas_call(
        functools.partial(_scan_both_kernel, tc=tc, nb=nb),
        grid=(2,),
        in_specs=[
            pl.BlockSpec((nb, tc, D_MODEL), lambda d: (0, 0, 0)),
            pl.BlockSpec((None, CONV_W, D_MODEL), lambda d: (layer, 0, 0)),
            pl.BlockSpec((None, 1, D_MODEL), lambda d: (layer, 0, 0)),
            gate_w, gate_w, vec, vec, vec, state,
        ],
        out_specs=[
            pl.BlockSpec((nb, tc, D_MODEL), lambda d: (0, 0, 0)),
            state,
        ],
        out_shape=[
            jax.ShapeDtypeStruct((nb, seq, D_MODEL), BF16),
            jax.ShapeDtypeStruct((2, nb, N_SLABS, LANES), F32),
        ],
        scratch_shapes=[pltpu.VMEM((tc + 2 * SUBLANES, D_MODEL), F32)]
        + [pltpu.VMEM((nb, tc * N_SLABS, LANES), F32)] * 3 + [pltpu.VMEM((nb, tc, D_MODEL), F32)],
        compiler_params=pltpu.CompilerParams(vmem_limit_bytes=_vmem_limit(56 << 20)),
        name="scan_ctx",
    )(xa, cw, cb, wa_bd, wx_bd, ba, bx, lam, h0)


def _attn_kernel(sink_ref, q_ref, *refs, layer, n_steps, qb, band):
    if band:
        kp_ref, kcur_ref, kn_ref, vp_ref, vcur_ref, vn_ref, kx_ref, vx_ref, o_ref = refs
    else:
        kx_ref, vx_ref, o_ref = refs
    n = pl.program_id(1)
    lane = lax.broadcasted_iota(jnp.int32, (1, LANES), 1)
    low = lane < HEAD_DIM
    high = jnp.logical_not(low)

    def block(j):
        return slice(j * ATT_BLOCK, (j + 1) * ATT_BLOCK)

    if band:
        kj = lax.broadcasted_iota(jnp.int32, (ATT_BLOCK, ATT_BLOCK), 0)
        qi = lax.broadcasted_iota(jnp.int32, (ATT_BLOCK, ATT_BLOCK), 1)

        def tiled(visible):
            one = jnp.where(visible, 0.0, NEG_BIG).astype(F32)
            return jnp.concatenate([one, one], axis=1)

        bias_prev = [tiled((kj >= qi) & (n > 0)) if j == 0 else tiled(kj >= qi) for j in range(qb)]
        bias_next = [tiled((kj <= qi) & (n < n_steps - 1)) if j == qb - 1 else tiled(kj <= qi) for j in range(qb)]

    def split(chain):
        j, rest = divmod(chain, 2 * N_KV_HEADS)
        kh, half = divmod(rest, 2)
        return j, kh, half

    def scores(chain):
        j, kh, half = split(chain)
        q2 = jnp.concatenate([q_ref[block(j), (2 * kh) * LANES:(2 * kh + 1) * LANES],
                              q_ref[block(j), (2 * kh + 1) * LANES:(2 * kh + 2) * LANES]], axis=0)
        q2 = jnp.where(low if half == 0 else high, q2, jnp.zeros_like(q2))
        slab = slice(kh * LANES, (kh + 1) * LANES)
        if band:
            blocks = [kp_ref[:, slab]] + [kcur_ref[block(i), slab] for i in range(qb)] + [kn_ref[:, slab]]
            kall = jnp.concatenate(blocks[j:j + 3] + [kx_ref[:, slab]], axis=0)
        else:
            kall = kx_ref[:, slab]
        return lax.dot_general(kall, q2, (((1,), (1,)), ((), ())), preferred_element_type=F32)

    def weights(chain, logits):
        j, kh, half = split(chain)
        pieces = [logits[i * ATT_BLOCK:(i + 1) * ATT_BLOCK, :] for i in range(logits.shape[0] // ATT_BLOCK)]
        if band:
            pieces[0] = pieces[0] + bias_prev[j]
            pieces[2] = pieces[2] + bias_next[j]
        h_top = 4 * kh + half
        sk = jnp.concatenate([jnp.full((1, LANES), sink_ref[layer, h_top] * LOG2E, F32),
                              jnp.full((1, LANES), sink_ref[layer, h_top + 2] * LOG2E, F32)], axis=1)
        widest = pieces[0]
        for piece in pieces[1:]:
            widest = jnp.maximum(widest, piece)
        m = jnp.maximum(sk, jnp.max(widest, axis=0, keepdims=True))
        p = jnp.concatenate([jnp.exp2(piece - m).astype(BF16) for piece in pieces], axis=0)
        return p, jnp.exp2(sk - m)

    def values(chain, p):
        j, kh, half = split(chain)
        rows = slice(kh * LANES, (kh + 1) * LANES)
        if band:
            blocks = [vp_ref[rows, :]] + [vcur_ref[rows, block(i)] for i in range(qb)] + [vn_ref[rows, :]]
            vall = jnp.concatenate(blocks[j:j + 3] + [vx_ref[rows, :]], axis=1)
        else:
            vall = vx_ref[rows, :]
        return _dot(vall, p)

    def finish(j, kh, pv, sink_p):
        res = [pv[h][:HEAD_DIM, :] / (pv[h][HEAD_DIM:HEAD_DIM + 1, :] + sink_p[h]) for h in range(2)]
        for i in range(2):
            cols = slice(i * ATT_BLOCK, (i + 1) * ATT_BLOCK)
            both = jnp.concatenate([res[0][:, cols], res[1][:, cols]], axis=0)
            o_ref[block(j), (2 * kh + i) * LANES:(2 * kh + i + 1) * LANES] = both.T.astype(BF16)

    n_chains = qb * 2 * N_KV_HEADS
    logits = {c: scores(c) for c in range(min(ATT_LOOKAHEAD, n_chains))}
    probs, sink_p, pv = {}, {}, {}
    for step in range(n_chains + ATT_VALUE_LAG):
        if step + ATT_LOOKAHEAD < n_chains:
            logits[step + ATT_LOOKAHEAD] = scores(step + ATT_LOOKAHEAD)
        if step < n_chains:
            probs[step], sink_p[step] = weights(step, logits.pop(step))
        c = step - ATT_VALUE_LAG
        if c >= 0:
            pv[c] = values(c, probs.pop(c))
            if c % 2 == 1:
                j, kh, _ = split(c)
                finish(j, kh, [pv.pop(c - 1), pv.pop(c)], [sink_p.pop(c - 1), sink_p.pop(c)])


def _attention(layer, sink, q, kd, vt, kdx, vtx, *, band, qb=ATT_QB):
    nb, seq, _ = q.shape
    nctx = kdx.shape[1]
    nblk = seq // ATT_BLOCK
    qb = min(qb, nblk)
    assert nblk % qb == 0
    n_steps = nblk // qb
    kvw = kdx.shape[-1]
    qspec = pl.BlockSpec((None, qb * ATT_BLOCK, D_MODEL), lambda b, n: (b, n, 0))
    in_specs = [pl.BlockSpec(memory_space=pltpu.SMEM), qspec]
    args = [sink, q]
    if band:
        first = lambda b, n: jnp.maximum(n * qb - 1, 0)
        last = lambda b, n: jnp.minimum((n + 1) * qb, nblk - 1)
        in_specs += [
            pl.BlockSpec((None, ATT_BLOCK, kvw), lambda b, n: (b, first(b, n), 0)),
            pl.BlockSpec((None, qb * ATT_BLOCK, kvw), lambda b, n: (b, n, 0)),
            pl.BlockSpec((None, ATT_BLOCK, kvw), lambda b, n: (b, last(b, n), 0)),
            pl.BlockSpec((kvw, ATT_BLOCK), lambda b, n: (0, b * nblk + first(b, n))),
            pl.BlockSpec((kvw, qb * ATT_BLOCK), lambda b, n: (0, b * n_steps + n)),
            pl.BlockSpec((kvw, ATT_BLOCK), lambda b, n: (0, b * nblk + last(b, n))),
        ]
        args += [kd, kd, kd, vt, vt, vt]
    in_specs += [pl.BlockSpec((None, nctx, kvw), lambda b, n: (b, 0, 0)),
                 pl.BlockSpec((kvw, nctx), lambda b, n: (0, b))]
    args += [kdx, vtx]
    return pl.pallas_call(
        functools.partial(_attn_kernel, layer=layer, n_steps=n_steps, qb=qb, band=band),
        grid=(nb, n_steps),
        in_specs=in_specs,
        out_specs=pl.BlockSpec((None, qb * ATT_BLOCK, D_MODEL), lambda b, n: (b, n, 0)),
        out_shape=jax.ShapeDtypeStruct((nb, seq, D_MODEL), BF16),
        compiler_params=pltpu.CompilerParams(vmem_limit_bytes=_vmem_limit(40 << 20)),
        name="attention_band" if band else "attention_ctx",
    )(*args)


def _merge_mlp_kernel(x_ref, mod_ref, g1_ref, g2_ref, wga_ref, wgb_ref, ya_ref, yb_ref, yc_ref, wb_ref, wo_ref,
                      w1_ref, w2_ref, o_ref, *, mod_base, tiles_per_row):
    i = pl.program_id(0)
    row = mod_base + i // tiles_per_row

    def mod(k):
        return mod_ref[pl.ds(row, 1), k * D_MODEL:(k + 1) * D_MODEL]

    def gate_logits(h1, k):
        parts = []
        for ref, base in ((wga_ref, 0), (wgb_ref, GATE_BLK)):
            lo = max(k * D_MODEL, base) - base
            hi = min((k + 1) * D_MODEL, base + GATE_BLK) - base
            if hi > lo:
                parts.append(_dot(h1, ref[:, lo:hi]))
        return parts[0] if len(parts) == 1 else jnp.concatenate(parts, axis=1)

    subs = [slice(j * MERGE_SUB, (j + 1) * MERGE_SUB) for j in range(x_ref.shape[0] // MERGE_SUB)]
    xs = [x_ref[r, :] for r in subs]
    ms = []
    for j, r in enumerate(subs):
        h1 = (_rms(xs[j], g1_ref[...] * (1.0 + mod(1))) + mod(0)).astype(BF16)
        branches = (ya_ref[r, :], yb_ref[r, :], yc_ref[r, :])
        m = None
        for k in range(3):
            gate = jax.nn.sigmoid(gate_logits(h1, k))
            term = gate * _dot(branches[k], wb_ref[k])
            m = term if m is None else m + term
        ms.append(m.astype(BF16))
    x1s = [xs[j] + mod(2) * _dot(ms[j], wo_ref[...]) for j in range(len(subs))]
    h2s = [(_rms(x1, g2_ref[...] * (1.0 + mod(4))) + mod(3)).astype(BF16) for x1 in x1s]
    acc = list(x1s)
    ff = D_FF // FF_CHUNKS
    for c in range(FF_CHUNKS):
        fs = [jnp.maximum(_dot(h2, w1_ref[:, c * ff:(c + 1) * ff]), 0.0) for h2 in h2s]
        for j in range(len(subs)):
            acc[j] = acc[j] + mod(5) * _dot((fs[j] * fs[j]).astype(BF16), w2_ref[c * ff:(c + 1) * ff, :])
    for j, r in enumerate(subs):
        o_ref[r, :] = acc[j]


def _merge_mlp(layer, x2d, mod, g1, g2, wg, ya, yb, yc, wb, wo, w1, w2, *, mod_base, tiles_per_row, tm):
    n_tok = x2d.shape[0]
    const = lambda i: (0, 0)
    lay2 = lambda i: (layer, 0, 0)
    once = pl.Buffered(1)
    tok = lambda i: (i, 0)
    return pl.pallas_call(
        functools.partial(_merge_mlp_kernel, mod_base=mod_base, tiles_per_row=tiles_per_row),
        grid=(n_tok // tm,),
        in_specs=[
            pl.BlockSpec((tm, D_MODEL), tok),
            pl.BlockSpec((None, MOD_ROWS, 6 * D_MODEL), lay2),
            pl.BlockSpec((None, 1, D_MODEL), lay2),
            pl.BlockSpec((None, 1, D_MODEL), lay2),
            pl.BlockSpec((None, D_MODEL, GATE_BLK), lambda i: (layer, 0, OFF_G // GATE_BLK), pipeline_mode=once),
            pl.BlockSpec((None, D_MODEL, GATE_BLK), lambda i: (layer, 0, OFF_G // GATE_BLK + 1), pipeline_mode=once),
            pl.BlockSpec((tm, D_MODEL), tok),
            pl.BlockSpec((tm, D_MODEL), tok),
            pl.BlockSpec((tm, D_MODEL), tok),
            pl.BlockSpec((None, 3, D_MODEL, D_MODEL), lambda i: (layer, 0, 0, 0), pipeline_mode=once),
            pl.BlockSpec((None, D_MODEL, D_MODEL), lay2, pipeline_mode=once),
            pl.BlockSpec((None, D_MODEL, D_FF), lay2, pipeline_mode=once),
            pl.BlockSpec((None, D_FF, D_MODEL), lay2, pipeline_mode=once),
        ],
        out_specs=pl.BlockSpec((tm, D_MODEL), tok),
        out_shape=jax.ShapeDtypeStruct((n_tok, D_MODEL), F32),
        compiler_params=pltpu.CompilerParams(vmem_limit_bytes=_vmem_limit(60 << 20)),
        name="merge_mlp",
    )(x2d, mod, g1, g2, wg, wg, ya, yb, yc, wb, wo, w1, w2)


def _rope_tables(seq):
    pos = jnp.arange(seq)
    row = (pos // GRID_W).astype(F32)
    col = (pos % GRID_W).astype(F32)
    inv = jnp.power(ROPE_BASE, -jnp.arange(ROPE_FREQS, dtype=F32) / ROPE_FREQS)
    ang_r = row[:, None] * inv
    ang_c = col[:, None] * inv
    cos = jnp.concatenate([jnp.cos(ang_r), jnp.cos(ang_r), jnp.cos(ang_c), jnp.cos(ang_c)], axis=-1)
    sin = jnp.concatenate([-jnp.sin(ang_r), jnp.sin(ang_r), -jnp.sin(ang_c), jnp.sin(ang_c)], axis=-1)
    reps = LANES // HEAD_DIM
    return jnp.tile(cos, (1, reps)), jnp.tile(sin, (1, reps))


def _block_diag(w):
    depth = w.shape[0]
    per = MXU_DIM // RNN_BLOCK
    rows = w.reshape(depth, 2, RNN_BLOCKS // per, MXU_DIM, RNN_BLOCK)
    tiled = jnp.tile(rows, (1, 1, 1, 1, per))
    blk = np.arange(MXU_DIM) // RNN_BLOCK
    return jnp.where(blk[:, None] == blk[None, :], tiled, 0.0).astype(BF16)


def kernel(x, c, ctx, c_ctx, w_mod, b_mod, g_norm1, w_in, conv_w, conv_b, lru_wa, lru_ba, lru_wx, lru_bx,
           lru_lambda, sgu_ln_g, sgu_ln_b, sgu_w, sgu_b, q_norm_g, k_norm_g, sink, w_branch, w_out, g_norm2,
           w_ff1, w_ff2):
    n_batch, n_tok, _ = x.shape
    n_ctx = ctx.shape[1]
    depth = w_mod.shape[0]
    tm_x, tm_c = 512, 512
    assert n_batch + 1 <= MOD_ROWS and n_tok % tm_x == 0 and (n_batch * n_ctx) % tm_c == 0

    cond = jnp.zeros((MOD_ROWS, D_MODEL), F32).at[:n_batch].set(c).at[n_batch].set(c_ctx)
    mod = _modulation(cond, w_mod, b_mod)

    cos, sin = _rope_tables(n_tok)
    q_mul = ATT_SCALE * LOG2E
    tabs_x = (cos * q_mul, sin * q_mul, cos, sin)
    one = jnp.ones((n_batch * n_ctx, LANES), F32)
    zero = jnp.zeros((n_batch * n_ctx, LANES), F32)
    tabs_c = (one * q_mul, zero, one, zero)

    head = np.arange(MXU_DIM) // HEAD_DIM
    ones_bd = jnp.asarray((head[:, None] == head[None, :]) * (1.0 / HEAD_DIM), BF16)

    w_in16 = w_in.astype(BF16)
    wb16 = w_branch.astype(BF16)
    wo16 = w_out.astype(BF16)
    w116 = w_ff1.astype(BF16)
    w216 = w_ff2.astype(BF16)
    ws16 = sgu_w.astype(BF16)
    wa_bd = _block_diag(lru_wa)
    wx_bd = _block_diag(lru_wx)
    ba_h = (0.5 * lru_ba)[:, :, None]
    bx_h = (0.5 * lru_bx)[:, :, None]
    lam = lru_lambda[:, :, None]
    conv_wh = 0.5 * conv_w
    conv_bh = (0.5 * conv_b)[:, None]

    x2d = x.reshape(n_batch * n_tok, D_MODEL)
    cx2d = ctx.reshape(n_batch * n_ctx, D_MODEL)
    h0 = jnp.zeros((2, n_batch, N_SLABS, LANES), F32)
    reps = LANES // HEAD_DIM

    def rglru(layer, xa, init):
        scan_p = (wa_bd, wx_bd, ba_h, bx_h, lam)
        hf, hf_fin, xc = _scan(layer, 0, xa, None, conv_wh, conv_bh, *scan_p, init)
        ya, hb_fin = _scan(layer, 1, xc, hf, None, None, *scan_p, init)
        return ya, jnp.stack([hf_fin, hb_fin])

    g1 = g_norm1[:, None]
    g2 = g_norm2[:, None]
    lng = sgu_ln_g[:, None]
    lnb = sgu_ln_b[:, None]
    bs = jnp.broadcast_to(sgu_b[:, :, :, None], (depth, SGU_GROUPS, SGU_CHUNK, LANES))
    qg = jnp.tile(q_norm_g, (1, reps))[:, None]
    kg = jnp.tile(k_norm_g, (1, reps))[:, None]

    for l in range(depth):
        last = l == depth - 1
        inproj_p = (mod, g1, w_in16, lng, lnb, ws16, bs, qg, kg, ones_bd)
        merge_w = (wb16, wo16, w116, w216)

        xa_c, yb_c, q_c, kd_c, vt_c = _inproj(l, cx2d, *inproj_p, tabs_c, seq=n_batch * n_ctx, mod_base=n_batch,
                                              per_batch=False, tm=tm_c)
        ya_c, hfin_c = _scan_both(l, xa_c.reshape(n_batch, n_ctx, D_MODEL), conv_wh, conv_bh, wa_bd, wx_bd, ba_h, bx_h,
                                  lam, h0)
        kd_c = kd_c.reshape(n_batch, n_ctx, -1)

        xa, yb, q, kd, vt = _inproj(l, x2d, *inproj_p, tabs_x, seq=n_tok, mod_base=0, per_batch=True, tm=tm_x)
        ya_x, _ = rglru(l, xa.reshape(n_batch, n_tok, D_MODEL), hfin_c)
        yc = _attention(l, sink, q.reshape(n_batch, n_tok, D_MODEL), kd.reshape(n_batch, n_tok, -1), vt,
                        kd_c, vt_c, band=True)
        x2d = _merge_mlp(l, x2d, mod, g1, g2, w_in16, ya_x.reshape(n_batch * n_tok, D_MODEL), yb,
                         yc.reshape(n_batch * n_tok, D_MODEL), *merge_w,
                         mod_base=0, tiles_per_row=n_tok // tm_x, tm=tm_x)

        if not last:
            yc_c = _attention(l, sink, q_c.reshape(n_batch, n_ctx, D_MODEL), None, None, kd_c, vt_c, band=False)
            cx2d = _merge_mlp(l, cx2d, mod, g1, g2, w_in16, ya_c.reshape(n_batch * n_ctx, D_MODEL), yb_c,
                              yc_c.reshape(n_batch * n_ctx, D_MODEL), *merge_w,
                              mod_base=n_batch, tiles_per_row=n_batch * n_ctx // tm_c, tm=tm_c)

    return x2d.reshape(n_batch, n_tok, D_MODEL)
```
